```python
import math
import jax, jax.numpy as jnp
from jax import lax
import numpy as np

D_MODEL = 1024
BATCH = 16
SEQ = 2048
DEPTH = 2
DEC_BATCH = 32
DEC_SEQ = 8
PAST_LEN = 16384
PAGE_SIZE = 128

N_A_LAYERS = DEPTH // 2
N_B_LAYERS = DEPTH - N_A_LAYERS
SSM_GROUP_WIDTH = 16
SSM_GROUPS = D_MODEL // SSM_GROUP_WIDTH
SSM_STATE = 64
SSM_CHUNK = 128
DT_MIN = 0.001
DT_MAX = 0.1
N_HEADS = 16
HEAD_DIM = 64
N_KV = 4
HEADS_PER_KV = N_HEADS // N_KV
N_BRANCH = 3
CMP_BLOCK = 32
CMP_STRIDE = 16
CMP_HIDDEN = 256
SEL_BLOCK = 64
N_SELECT = 16
WINDOW = 512
Q_BLOCK = 128
ATTN_SCALE = HEAD_DIM ** -0.5
FORCE_SCORE = 1e4
NEG_INF = -1e30
TINY = 1e-30
N_EXPERT_GROUPS = 4
EXPERTS_PER_GROUP = 8
N_EXPERTS = N_EXPERT_GROUPS * EXPERTS_PER_GROUP
TOP_K_IN_GROUP = 2
D_EXPERT = 256
MOE_ROW_BLOCK = 128
DEEPNORM_ALPHA = (2 * DEPTH) ** 0.25
DEEPNORM_BETA = (8 * DEPTH) ** -0.25
LN_EPS = 1e-5

kernel_name = 'yoco_s5_nsa_hmoe_step'


def layer_norm(x, g, b):
    xf = x.astype(jnp.float32)
    mu = jnp.mean(xf, -1, keepdims=True)
    var = jnp.mean(jnp.square(xf - mu), -1, keepdims=True)
    y = (xf - mu) * lax.rsqrt(var + LN_EPS) * g.astype(jnp.float32) + b.astype(jnp.float32)
    return y.astype(x.dtype)


def deepnorm_residual(x, f, g, b):
    return layer_norm(DEEPNORM_ALPHA * x + f.astype(x.dtype), g, b)


def masked_softmax(s, mask):
    s = jnp.where(mask, s.astype(jnp.float32), NEG_INF)
    p = jnp.exp(s - jnp.max(s, -1, keepdims=True)) * mask
    return p / jnp.maximum(jnp.sum(p, -1, keepdims=True), TINY)


def _cmul(ar, ai, br, bi):
    return ar * br - ai * bi, ar * bi + ai * br


def _ssm_combine(e1, e2):
    a1r, a1i, b1r, b1i = e1
    a2r, a2i, b2r, b2i = e2
    ar, ai = _cmul(a2r, a2i, a1r, a1i)
    cr, ci = _cmul(a2r, a2i, b1r, b1i)
    return ar, ai, cr + b2r, ci + b2i


def s5_discretize(lam_re, lam_im, log_dt, b_re, b_im):
    lr, li = lam_re.astype(jnp.float32), lam_im.astype(jnp.float32)
    dt = jnp.exp(log_dt.astype(jnp.float32))[:, None]
    mag = jnp.exp(lr * dt)
    abar_re, abar_im = mag * jnp.cos(li * dt), mag * jnp.sin(li * dt)
    den = lr * lr + li * li
    nr = abar_re - 1.0
    z_re = (nr * lr + abar_im * li) / den
    z_im = (abar_im * lr - nr * li) / den
    bbar_re, bbar_im = _cmul(z_re[..., None], z_im[..., None],
                             b_re.astype(jnp.float32), b_im.astype(jnp.float32))
    return abar_re, abar_im, bbar_re, bbar_im


def s5_mixer(x, h0, w_in, lam_re, lam_im, log_dt, b_re, b_im, c_re, c_im, d_skip, w_glu):
    bsz, L, _ = x.shape
    u = x @ w_in
    abar_re, abar_im, bb_re, bb_im = s5_discretize(lam_re, lam_im, log_dt, b_re, b_im)
    cr_w, ci_w = c_re.astype(jnp.float32), c_im.astype(jnp.float32)
    chunk = SSM_CHUNK if L % SSM_CHUNK == 0 else L
    n_chunks = L // chunk
    uc = u.astype(jnp.float32).reshape(bsz, n_chunks, chunk, SSM_GROUPS, SSM_GROUP_WIDTH)
    uc = jnp.moveaxis(uc, 1, 0)

    def step(h, u_blk):
        hr, hi = h
        br = jnp.einsum('btgw,gpw->btgp', u_blk, bb_re)
        bi = jnp.einsum('btgw,gpw->btgp', u_blk, bb_im)
        ar_h, ai_h = _cmul(abar_re, abar_im, hr, hi)
        br = br.at[:, 0].add(ar_h)
        bi = bi.at[:, 0].add(ai_h)
        a_re_b = jnp.broadcast_to(abar_re, br.shape)
        a_im_b = jnp.broadcast_to(abar_im, bi.shape)
        _, _, sr, si = lax.associative_scan(_ssm_combine, (a_re_b, a_im_b, br, bi), axis=1)
        y = jnp.einsum('btgp,gwp->btgw', sr, cr_w) - jnp.einsum('btgp,gwp->btgw', si, ci_w)
        return (sr[:, -1], si[:, -1]), y

    h_init = (h0[..., 0].astype(jnp.float32), h0[..., 1].astype(jnp.float32))
    (hr, hi), yc = lax.scan(step, h_init, uc)
    y = jnp.moveaxis(yc, 0, 1).reshape(bsz, L, D_MODEL)
    y = y + d_skip.astype(jnp.float32) * u.astype(jnp.float32)
    g = jax.nn.gelu(y).astype(x.dtype)
    val, gate = jnp.split(g @ w_glu, 2, axis=-1)
    return val * jax.nn.sigmoid(gate), jnp.stack([hr, hi], axis=-1)


def shared_kv(h, kv_w):
    bsz, L, _ = h.shape
    return (h @ kv_w).reshape(bsz, L, N_BRANCH, 2, N_KV, HEAD_DIM)


def compress_blocks(rows, cmp_pe, cmp_w1, cmp_w2):
    bsz, T = rows.shape[:2]
    n_sub = T // CMP_STRIDE
    sub = rows[:, :n_sub * CMP_STRIDE].reshape(bsz, n_sub, CMP_STRIDE, 2, N_KV, HEAD_DIM)
    w1 = cmp_w1.reshape(2, CMP_BLOCK // CMP_STRIDE, CMP_STRIDE, HEAD_DIM, CMP_HIDDEN)
    lo = jnp.einsum('bnskgd,ksdh->bnkgh', sub, w1[:, 0])
    hi = jnp.einsum('bnskgd,ksdh->bnkgh', sub, w1[:, 1])
    pe_bias = jnp.einsum('kf,kfh->kh', cmp_pe.reshape(2, CMP_BLOCK * HEAD_DIM), cmp_w1)
    hid = jax.nn.gelu(lo[:, :-1] + hi[:, 1:] + pe_bias[:, None, :])
    return jnp.einsum('bnkgh,khd->bnkgd', hid, cmp_w2)


def kv_side_prompt(h, kv_w, cmp_pe, cmp_w1, cmp_w2):
    kv = shared_kv(h, kv_w)
    return kv, compress_blocks(kv[:, :, 0], cmp_pe, cmp_w1, cmp_w2)


def kv_side_sample(h, cache_cmp_kv, cache_win_kv, page_table, kv_w, cmp_pe, cmp_w1, cmp_w2):
    kv_new = shared_kv(h, kv_w)
    db = h.shape[0]
    past = page_table.shape[1] * cache_cmp_kv.shape[1]
    past_cmp = cache_cmp_kv[page_table].reshape((db, past) + cache_cmp_kv.shape[2:])
    full_cmp = jnp.concatenate([past_cmp, kv_new[:, :, 0].astype(past_cmp.dtype)], axis=1)
    ckv = compress_blocks(full_cmp, cmp_pe, cmp_w1, cmp_w2)
    win_all = jnp.concatenate([cache_win_kv, kv_new[:, :, 2].astype(cache_win_kv.dtype)], axis=1)
    return kv_new, ckv, win_all


def gather_paged(pool, page_table, new_rows, tok):
    db = page_table.shape[0]
    page = pool.shape[1]
    past = page_table.shape[1] * page
    n_new = new_rows.shape[1]
    b = jnp.arange(db).reshape((db,) + (1,) * (tok.ndim - 1))
    g = jnp.arange(N_KV)[:, None]
    tp = jnp.minimum(tok, past - 1)
    phys = page_table[b, tp // page]
    from_past = pool[phys, tp % page, :, g]
    from_new = new_rows[b, jnp.clip(tok - past, 0, n_new - 1), :, g]
    return jnp.where((tok < past)[..., None, None], from_past, from_new.astype(from_past.dtype))


def query_side(x, w_qg):
    bsz, L, _ = x.shape
    qg = x @ w_qg
    q = qg[..., :N_HEADS * HEAD_DIM].reshape(bsz, L, N_KV, HEADS_PER_KV, HEAD_DIM)
    gates = jax.nn.sigmoid(qg[..., N_HEADS * HEAD_DIM:].astype(jnp.float32))
    return q, gates.reshape(bsz, L, N_KV, HEADS_PER_KV, N_BRANCH)


def cmp_attend(q, q_pos, ckv):
    n_cmp = ckv.shape[1]
    c_end = jnp.arange(n_cmp) * CMP_STRIDE + CMP_BLOCK - 1
    mask = (c_end[None, :] <= q_pos[:, None])[None, :, None, None, :]
    s = jnp.einsum('blgnd,bcgd->blgnc', q, ckv[:, :, 0]) * ATTN_SCALE
    p = masked_softmax(s, mask)
    o = jnp.einsum('blgnc,bcgd->blgnd', p.astype(ckv.dtype), ckv[:, :, 1])
    return o, p


def select_blocks(p_cmp, q_pos, n_sel):
    pg = jnp.moveaxis(jnp.sum(p_cmp, axis=3), -1, 0)
    c = jnp.arange(pg.shape[0])
    j_first = (c * CMP_STRIDE) // SEL_BLOCK
    j_last = (c * CMP_STRIDE + CMP_BLOCK - 1) // SEL_BLOCK
    spill = (j_last != j_first).astype(pg.dtype)[:, None, None, None]
    ps = (jax.ops.segment_sum(pg, j_first, num_segments=n_sel)
          + jax.ops.segment_sum(pg * spill, j_last, num_segments=n_sel))
    ps = jnp.moveaxis(ps, 0, -1)
    j = jnp.arange(n_sel)
    valid = (j[None, :] * SEL_BLOCK <= q_pos[:, None])[None, :, None, :]
    forced = ((j[None, :] == (q_pos // SEL_BLOCK)[:, None]) | (j[None, :] == 0))[None, :, None, :]
    score = jnp.where(valid, ps + FORCE_SCORE * forced.astype(ps.dtype), -1.0)
    _, idx = lax.top_k(score, min(N_SELECT, n_sel))
    return idx


def block_tokens(blk):
    tok = blk[..., None] * SEL_BLOCK + jnp.arange(SEL_BLOCK)
    return tok.reshape(blk.shape[:-1] + (-1,))


def slc_attend(q, q_pos, tok, k_rows, v_rows):
    s = jnp.einsum('...gnd,...gkd->...gnk', q, k_rows) * ATTN_SCALE
    mask = (tok <= q_pos[..., None, None])[..., None, :]
    p = masked_softmax(s, mask)
    return jnp.einsum('...gnk,...gkd->...gnd', p.astype(v_rows.dtype), v_rows)


def win_attend(q, q_pos, k_pos, k, v):
    dist = q_pos[:, None] - k_pos[None, :]
    mask = ((dist >= 0) & (dist < WINDOW) & (k_pos[None, :] >= 0))[None, :, None, None, :]
    s = jnp.einsum('blgnd,bkgd->blgnk', q, k) * ATTN_SCALE
    p = masked_softmax(s, mask)
    return jnp.einsum('blgnk,bkgd->blgnd', p.astype(v.dtype), v)


def nsa_merge(o_cmp, o_slc, o_win, gates, w_o):
    g = gates.astype(o_cmp.dtype)
    o = g[..., 0:1] * o_cmp + g[..., 1:2] * o_slc.astype(o_cmp.dtype) + g[..., 2:3] * o_win.astype(o_cmp.dtype)
    bsz, L = o.shape[:2]
    return o.reshape(bsz, L, N_HEADS * HEAD_DIM) @ w_o


def nsa_prompt(x, kv, ckv, w_qg, w_o):
    bsz, L, _ = x.shape
    q, gates = query_side(x, w_qg)
    q_pos = jnp.arange(L)
    o_cmp, p_cmp = cmp_attend(q, q_pos, ckv)
    blk = select_blocks(p_cmp, q_pos, -(-L // SEL_BLOCK))
    qb = min(Q_BLOCK, L)
    nqb = L // qb
    slc_k = jnp.swapaxes(kv[:, :, 1, 0], 1, 2)
    slc_v = jnp.swapaxes(kv[:, :, 1, 1], 1, 2)
    g_ix = jnp.arange(N_KV)[:, None]

    def slc_block(args):
        b, q_blk, p_blk, b_blk = args
        tok = block_tokens(b_blk)
        tc = jnp.minimum(tok, L - 1)
        return slc_attend(q_blk, p_blk, tok, slc_k[b, g_ix, tc], slc_v[b, g_ix, tc])

    o_slc = lax.map(slc_block, (jnp.repeat(jnp.arange(bsz), nqb),
                                q.reshape(bsz * nqb, qb, N_KV, HEADS_PER_KV, HEAD_DIM),
                                jnp.tile(q_pos.reshape(nqb, qb), (bsz, 1)),
                                blk.reshape(bsz * nqb, qb, N_KV, -1))).reshape(q.shape)
    win = jnp.pad(kv[:, :, 2], ((0, 0), (WINDOW, 0), (0, 0), (0, 0), (0, 0)))

    def win_block(c):
        start = c * qb
        kw = lax.dynamic_slice_in_dim(win, start, qb + WINDOW, axis=1)
        q_blk = lax.dynamic_slice_in_dim(q, start, qb, axis=1)
        return win_attend(q_blk, start + jnp.arange(qb), start - WINDOW + jnp.arange(qb + WINDOW),
                          kw[:, :, 0], kw[:, :, 1])

    o_win = jnp.moveaxis(lax.map(win_block, jnp.arange(nqb)), 0, 1).reshape(q.shape)
    return nsa_merge(o_cmp, o_slc, o_win, gates, w_o)


def nsa_sample(x, kv_new, ckv, win_all, cache_slc_kv, page_table, w_qg, w_o):
    n_new = x.shape[1]
    past = page_table.shape[1] * cache_slc_kv.shape[1]
    total = past + n_new
    q, gates = query_side(x, w_qg)
    q_pos = past + jnp.arange(n_new)
    o_cmp, p_cmp = cmp_attend(q, q_pos, ckv)
    blk = select_blocks(p_cmp, q_pos, -(-total // SEL_BLOCK))
    tok = block_tokens(blk)
    rows = gather_paged(cache_slc_kv, page_table, kv_new[:, :, 1], tok)
    o_slc = slc_attend(q, q_pos, tok, rows[..., 0, :], rows[..., 1, :])
    n_win = win_all.shape[1]
    k_pos = past - (n_win - n_new) + jnp.arange(n_win)
    o_win = win_attend(q, q_pos, k_pos, win_all[:, :, 0], win_all[:, :, 1])
    return nsa_merge(o_cmp, o_slc, o_win, gates, w_o)


def hier_moe(x, w_group, w_expert, w_gate_up, w_down):
    shp = x.shape
    xt = x.reshape(-1, shp[-1])
    T = xt.shape[0]
    lg = (xt @ w_group).astype(jnp.float32)
    g_top, g_sel = lax.top_k(lg, 1)
    g_w = jnp.exp(g_top[:, 0] - jax.nn.logsumexp(lg, axis=-1))
    le = (xt @ w_expert).astype(jnp.float32).reshape(T, N_EXPERT_GROUPS, EXPERTS_PER_GROUP)
    le_g = le[jnp.arange(T), g_sel[:, 0]]
    e_top, e_loc = lax.top_k(le_g, TOP_K_IN_GROUP)
    w = g_w[:, None] * jax.nn.softmax(e_top, axis=-1)
    e_id = g_sel * EXPERTS_PER_GROUP + e_loc
    n_asg = T * TOP_K_IN_GROUP
    e_flat = e_id.reshape(-1)
    tok_flat = jnp.repeat(jnp.arange(T), TOP_K_IN_GROUP)
    order = jnp.argsort(e_flat)
    e_s, tok_s, w_s = e_flat[order], tok_flat[order], w.reshape(-1)[order]
    counts = jax.ops.segment_sum(jnp.ones((n_asg,), jnp.int32), e_flat, num_segments=N_EXPERTS)
    starts = jnp.cumsum(counts) - counts
    padded = (counts + MOE_ROW_BLOCK - 1) // MOE_ROW_BLOCK * MOE_ROW_BLOCK
    pends = jnp.cumsum(padded)
    dest = (pends - padded)[e_s] + jnp.arange(n_asg) - starts[e_s]
    n_blocks = -(-n_asg // MOE_ROW_BLOCK) + N_EXPERTS
    n_rows = n_blocks * MOE_ROW_BLOCK
    row_tok = jnp.zeros((n_rows,), jnp.int32).at[dest].set(tok_s)
    row_w = jnp.zeros((n_rows,), jnp.float32).at[dest].set(w_s)
    blk_e = jnp.minimum(jnp.searchsorted(pends, jnp.arange(n_blocks) * MOE_ROW_BLOCK, side='right'),
                        N_EXPERTS - 1)

    def expert_block(args):
        e, toks = args
        gate, up = jnp.split(xt[toks] @ w_gate_up[e], 2, axis=-1)
        return (jax.nn.silu(gate) * up) @ w_down[e]

    yb = lax.map(expert_block, (blk_e, row_tok.reshape(n_blocks, MOE_ROW_BLOCK)))
    out = jnp.zeros((T, shp[-1]), x.dtype).at[row_tok].add(
        yb.reshape(n_rows, shp[-1]) * row_w[:, None].astype(x.dtype))
    return out.reshape(shp)


def setup_inputs(seed: int = 0) -> dict:
    key = jax.random.key(seed)
    ks = iter(jax.random.split(key, 40))

    def nrm(shape, scale):
        return jax.random.normal(next(ks), shape, jnp.float32) * scale

    n_pages = PAST_LEN // PAGE_SIZE
    n_pool = (5 * DEC_BATCH * n_pages + 3) // 4
    win_buf = min(WINDOW, PAST_LEN)
    x_prompt = nrm((BATCH, SEQ, D_MODEL), 1.0)
    x_sample = nrm((DEC_BATCH, DEC_SEQ, D_MODEL), 1.0)
    state_ssm = nrm((N_A_LAYERS, DEC_BATCH, SSM_GROUPS, SSM_STATE, 2), 0.3)
    cache_cmp_kv = nrm((n_pool, PAGE_SIZE, 2, N_KV, HEAD_DIM), 1.0)
    cache_slc_kv = nrm((n_pool, PAGE_SIZE, 2, N_KV, HEAD_DIM), 1.0)
    cache_win_kv = nrm((DEC_BATCH, win_buf, 2, N_KV, HEAD_DIM), 1.0)
    page_table = jax.random.permutation(next(ks), n_pool)[:DEC_BATCH * n_pages]
    page_table = page_table.reshape(DEC_BATCH, n_pages).astype(jnp.int32)
    a_w_in = nrm((N_A_LAYERS, D_MODEL, D_MODEL), D_MODEL ** -0.5)
    a_lambda_re = -0.5 + nrm((N_A_LAYERS, SSM_GROUPS, SSM_STATE), 0.01)
    a_lambda_im = math.pi * jnp.arange(SSM_STATE, dtype=jnp.float32) + nrm((N_A_LAYERS, SSM_GROUPS, SSM_STATE), 0.01)
    a_log_dt = math.log(DT_MIN) + jax.random.uniform(next(ks), (N_A_LAYERS, SSM_GROUPS), jnp.float32) * (math.log(DT_MAX) - math.log(DT_MIN))
    a_b_re = nrm((N_A_LAYERS, SSM_GROUPS, SSM_STATE, SSM_GROUP_WIDTH), (2 * SSM_GROUP_WIDTH) ** -0.5)
    a_b_im = nrm((N_A_LAYERS, SSM_GROUPS, SSM_STATE, SSM_GROUP_WIDTH), (2 * SSM_GROUP_WIDTH) ** -0.5)
    a_c_re = nrm((N_A_LAYERS, SSM_GROUPS, SSM_GROUP_WIDTH, SSM_STATE), SSM_STATE ** -0.5)
    a_c_im = nrm((N_A_LAYERS, SSM_GROUPS, SSM_GROUP_WIDTH, SSM_STATE), SSM_STATE ** -0.5)
    a_d = nrm((N_A_LAYERS, D_MODEL), 1.0)
    a_w_glu = jnp.concatenate([nrm((N_A_LAYERS, D_MODEL, D_MODEL), DEEPNORM_BETA * D_MODEL ** -0.5),
                               nrm((N_A_LAYERS, D_MODEL, D_MODEL), D_MODEL ** -0.5)], axis=-1)
    kv_w = nrm((D_MODEL, N_BRANCH * 2 * N_KV * HEAD_DIM), D_MODEL ** -0.5)
    cmp_pe = nrm((2, CMP_BLOCK, HEAD_DIM), 0.02)
    cmp_w1 = nrm((2, CMP_BLOCK * HEAD_DIM, CMP_HIDDEN), (CMP_BLOCK * HEAD_DIM) ** -0.5)
    cmp_w2 = nrm((2, CMP_HIDDEN, HEAD_DIM), CMP_HIDDEN ** -0.5)
    b_w_qg = nrm((N_B_LAYERS, D_MODEL, N_HEADS * HEAD_DIM + N_HEADS * N_BRANCH), D_MODEL ** -0.5)
    b_w_o = nrm((N_B_LAYERS, N_HEADS * HEAD_DIM, D_MODEL), DEEPNORM_BETA * (N_HEADS * HEAD_DIM) ** -0.5)
    moe_w_group = nrm((DEPTH, D_MODEL, N_EXPERT_GROUPS), D_MODEL ** -0.5)
    moe_w_expert = nrm((DEPTH, D_MODEL, N_EXPERTS), D_MODEL ** -0.5)
    moe_w_gate_up = nrm((DEPTH, N_EXPERTS, D_MODEL, 2 * D_EXPERT), D_MODEL ** -0.5)
    moe_w_down = nrm((DEPTH, N_EXPERTS, D_EXPERT, D_MODEL), DEEPNORM_BETA * D_EXPERT ** -0.5)
    ln_g = 1.0 + nrm((DEPTH, 2, D_MODEL), 0.02)
    ln_b = nrm((DEPTH, 2, D_MODEL), 0.02)
    return {'x_prompt': x_prompt, 'x_sample': x_sample, 'state_ssm': state_ssm,
            'cache_cmp_kv': cache_cmp_kv, 'cache_slc_kv': cache_slc_kv, 'cache_win_kv': cache_win_kv,
            'page_table': page_table,
            'a_w_in': a_w_in, 'a_lambda_re': a_lambda_re, 'a_lambda_im': a_lambda_im, 'a_log_dt': a_log_dt,
            'a_b_re': a_b_re, 'a_b_im': a_b_im, 'a_c_re': a_c_re, 'a_c_im': a_c_im, 'a_d': a_d,
            'a_w_glu': a_w_glu, 'kv_w': kv_w, 'cmp_pe': cmp_pe, 'cmp_w1': cmp_w1, 'cmp_w2': cmp_w2,
            'b_w_qg': b_w_qg, 'b_w_o': b_w_o, 'moe_w_group': moe_w_group, 'moe_w_expert': moe_w_expert,
            'moe_w_gate_up': moe_w_gate_up, 'moe_w_down': moe_w_down, 'ln_g': ln_g, 'ln_b': ln_b}


def reference(x_prompt, x_sample, state_ssm, cache_cmp_kv, cache_slc_kv, cache_win_kv, page_table,
              a_w_in, a_lambda_re, a_lambda_im, a_log_dt, a_b_re, a_b_im, a_c_re, a_c_im, a_d, a_w_glu,
              kv_w, cmp_pe, cmp_w1, cmp_w2, b_w_qg, b_w_o,
              moe_w_group, moe_w_expert, moe_w_gate_up, moe_w_down, ln_g, ln_b):
    xp, xs = x_prompt, x_sample
    ssm_p, ssm_s = [], []
    for layer in range(DEPTH):
        if layer < N_A_LAYERS:
            a_par = (a_w_in[layer], a_lambda_re[layer], a_lambda_im[layer], a_log_dt[layer],
                     a_b_re[layer], a_b_im[layer], a_c_re[layer], a_c_im[layer], a_d[layer], a_w_glu[layer])
            h0 = jnp.zeros((xp.shape[0], SSM_GROUPS, SSM_STATE, 2), jnp.float32)
            fp, hp = s5_mixer(xp, h0, *a_par)
            fs, hs = s5_mixer(xs, state_ssm[layer], *a_par)
            ssm_p.append(hp)
            ssm_s.append(hs.astype(state_ssm.dtype))
        else:
            if layer == N_A_LAYERS:
                kv_p, ckv_p = kv_side_prompt(xp, kv_w, cmp_pe, cmp_w1, cmp_w2)
                kv_s, ckv_s, win_s = kv_side_sample(xs, cache_cmp_kv, cache_win_kv, page_table,
                                                    kv_w, cmp_pe, cmp_w1, cmp_w2)
            bl = layer - N_A_LAYERS
            fp = nsa_prompt(xp, kv_p, ckv_p, b_w_qg[bl], b_w_o[bl])
            fs = nsa_sample(xs, kv_s, ckv_s, win_s, cache_slc_kv, page_table, b_w_qg[bl], b_w_o[bl])
        xp = deepnorm_residual(xp, fp, ln_g[layer, 0], ln_b[layer, 0])
        xs = deepnorm_residual(xs, fs, ln_g[layer, 0], ln_b[layer, 0])
        moe_par = (moe_w_group[layer], moe_w_expert[layer], moe_w_gate_up[layer], moe_w_down[layer])
        xp = deepnorm_residual(xp, hier_moe(xp, *moe_par), ln_g[layer, 1], ln_b[layer, 1])
        xs = deepnorm_residual(xs, hier_moe(xs, *moe_par), ln_g[layer, 1], ln_b[layer, 1])
    p_win = kv_p[:, -min(WINDOW, kv_p.shape[1]):, 2]
    s_win = win_s[:, -min(WINDOW, win_s.shape[1]):]
    return (xp, xs, jnp.stack(ssm_p), kv_p[:, :, 0], kv_p[:, :, 1], p_win,
            jnp.stack(ssm_s), kv_s[:, :, 0], kv_s[:, :, 1], s_win)
```

```python
import functools
import math

import jax
import jax.numpy as jnp
import numpy as np
from jax import lax
from jax.experimental import pallas as pl
from jax.experimental.pallas import tpu as pltpu

_F32 = jnp.float32
_MXU_DTYPE = jnp.bfloat16
_LANES = 128
_SUBLANES = 8
_VMEM_LIMIT = 56 * 1024 * 1024

_DEPTH = 2
_SSM_W = 16
_SSM_P = 64
_GB = _LANES // _SSM_W
_BSUB = 16
_N_HEADS = 16
_HEAD_DIM = 64
_N_KV = 4
_HPG = _N_HEADS // _N_KV
_N_BRANCH = 3
_CMP_BLOCK = 32
_CMP_STRIDE = 16
_CMP_HIDDEN = 256
_SEL_BLOCK = 64
_N_SELECT = 16
_WINDOW = 512
_ATTN_SCALE = _HEAD_DIM ** -0.5
_FORCE_SCORE = 1e4
_NEG_INF = -1e30
_TINY = 1e-30
_NG = 4
_EPG = 8
_NE = _NG * _EPG
_D_EXPERT = 256
_MOE_ROWS = 128
_ALPHA = (2 * _DEPTH) ** 0.25
_LN_EPS = 1e-5


def _mx(x):
    return x.astype(_MXU_DTYPE)


def _dot(a, b):
    return jnp.dot(_mx(a), _mx(b), preferred_element_type=_F32)


def _dot_nt(a, b):
    return lax.dot_general(_mx(a), _mx(b), (((1,), (1,)), ((), ())), preferred_element_type=_F32)


def _split2(a):
    a1 = _mx(a)
    return a1, _mx(a - a1.astype(_F32))


def _dot_hi(a, b):
    a1, a2 = _split2(a)
    b1, b2 = _split2(b)
    return _dot(a1, b1) + (_dot(a1, b2) + _dot(a2, b1))


def _params(*sem):
    return pltpu.CompilerParams(dimension_semantics=sem, vmem_limit_bytes=_VMEM_LIMIT)


def _layer_norm(v, g, b):
    mu = jnp.mean(v, -1, keepdims=True)
    d = v - mu
    var = jnp.mean(d * d, -1, keepdims=True)
    return d * lax.rsqrt(var + _LN_EPS) * g + b


def _proj_kernel(cols, acts, x_ref, w_ref, *o_refs):
    y = _dot(x_ref[...], w_ref[...])
    for (c0, c1), act, o_ref in zip(cols, acts, o_refs):
        v = y[:, c0:c1]
        if act == "sigmoid":
            v = jax.nn.sigmoid(v)
        o_ref[...] = v.astype(o_ref.dtype)


def _proj(x, w, tm, cols, dtypes, acts=None, grid=None, x_map=None, out_shapes=None, out_maps=None):
    m, k = x.shape
    n = w.shape[1]
    acts = acts or [None] * len(cols)
    if grid is None:
        grid = (m // tm,)
        x_map = lambda i: (i, 0)
        out_shapes = [(m, c1 - c0) for c0, c1 in cols]
        out_maps = [x_map] * len(cols)
    w_map = (lambda *a: (0, 0))
    return pl.pallas_call(
        functools.partial(_proj_kernel, tuple(cols), tuple(acts)),
        grid=grid,
        in_specs=[pl.BlockSpec((tm, k), x_map), pl.BlockSpec((k, n), w_map)],
        out_specs=[pl.BlockSpec((tm, c1 - c0), om) for (c0, c1), om in zip(cols, out_maps)],
        out_shape=[jax.ShapeDtypeStruct(s, d) for s, d in zip(out_shapes, dtypes)],
        compiler_params=_params(*(["arbitrary"] * len(grid))),
    )(x, _mx(w))


def _s5_disc_kernel(lr_ref, li_ref, ldt_ref, bre_ref, bim_ref, ar_ref, ai_ref, bbr_ref, bbi_ref):
    lr, li = lr_ref[...], li_ref[...]
    dt = jnp.exp(ldt_ref[...])
    mag = jnp.exp(lr * dt)
    a_re, a_im = mag * jnp.cos(li * dt), mag * jnp.sin(li * dt)
    den = lr * lr + li * li
    nr = a_re - 1.0
    z_re = (nr * lr + a_im * li) / den
    z_im = (a_im * lr - nr * li) / den
    b_re, b_im = bre_ref[...], bim_ref[...]
    ar_ref[...] = a_re
    ai_ref[...] = a_im
    bbr_ref[...] = z_re * b_re - z_im * b_im
    bbi_ref[...] = z_re * b_im + z_im * b_re


def _s5_discretize(lam_re, lam_im, log_dt, b_re, b_im):
    g, p = lam_re.shape
    w = b_re.shape[-1]
    n = g * p
    flat = lambda a: a.reshape(1, n)
    b_t = lambda a: a.reshape(n, w).T
    ldt = jnp.repeat(log_dt, p).reshape(1, n)
    outs = pl.pallas_call(
        _s5_disc_kernel,
        out_shape=[jax.ShapeDtypeStruct((1, n), _F32)] * 2 + [jax.ShapeDtypeStruct((w, n), _F32)] * 2,
    )(flat(lam_re), flat(lam_im), ldt, b_t(b_re), b_t(b_im))
    return outs


def _s5_block_weights(abar_re, abar_im, bbt_re, bbt_im, c_re, c_im):
    g = abar_re.shape[1] // _SSM_P
    nb = g // _GB
    eye = jnp.eye(_GB, dtype=_F32)

    def in_mat(bt):
        b4 = bt.reshape(_SSM_W, nb, _GB, _SSM_P)
        return jnp.einsum("wbgp,gh->bgwhp", b4, eye).reshape(nb, _GB * _SSM_W, _GB * _SSM_P)

    def out_mat(c):
        c4 = c.reshape(nb, _GB, _SSM_W, _SSM_P)
        return jnp.einsum("bgwp,gh->bgphw", c4, eye).reshape(nb, _GB * _SSM_P, _GB * _SSM_W)

    bblk = jnp.concatenate([in_mat(bbt_re), in_mat(bbt_im)], axis=2)
    cblk = jnp.concatenate([out_mat(c_re), -out_mat(c_im)], axis=1)
    a = jnp.concatenate([abar_re.reshape(nb, 1, _GB * _SSM_P), abar_im.reshape(nb, 1, _GB * _SSM_P)], axis=2)
    a = jnp.broadcast_to(a, (nb, _BSUB, 2 * _GB * _SSM_P))
    return _mx(bblk), _mx(cblk), a


def _state_to_blocks(h):
    b, g, p, _ = h.shape
    h6 = h.reshape(b // _BSUB, _BSUB, g // _GB, _GB, p, 2)
    return jnp.transpose(h6, (0, 2, 1, 5, 3, 4)).reshape(b // _BSUB, g // _GB, _BSUB, 2 * _GB * p)


def _blocks_to_state(hb):
    nbh, nb, _, _ = hb.shape
    h6 = hb.reshape(nbh, nb, _BSUB, 2, _GB, _SSM_P)
    return jnp.transpose(h6, (0, 2, 1, 4, 5, 3)).reshape(nbh * _BSUB, nb * _GB, _SSM_P, 2)


def _s5_scan_kernel(tc_len, u_ref, bblk_ref, cblk_ref, a_ref, d_ref, h0_ref, g_ref, ht_ref, hbuf, st):
    tci = pl.program_id(2)
    half = _GB * _SSM_P

    @pl.when(tci == 0)
    def _():
        st[...] = h0_ref[...]

    u = u_ref[...]
    hbuf[...] = _dot(u, bblk_ref[...])
    a_re, a_im = a_ref[:, :half], a_ref[:, half:]

    def step(t, carry):
        h_re, h_im = carry
        r0 = pl.multiple_of(t * _BSUB, _BSUB)
        n_re = a_re * h_re - a_im * h_im + hbuf[pl.ds(r0, _BSUB), :half]
        n_im = a_re * h_im + a_im * h_re + hbuf[pl.ds(r0, _BSUB), half:]
        hbuf[pl.ds(r0, _BSUB), :half] = n_re
        hbuf[pl.ds(r0, _BSUB), half:] = n_im
        return n_re, n_im

    h_re, h_im = lax.fori_loop(0, tc_len, step, (st[:, :half], st[:, half:]))
    st[:, :half] = h_re
    st[:, half:] = h_im
    y = _dot(hbuf[...], cblk_ref[...]) + d_ref[...] * u
    g_ref[...] = jax.nn.gelu(y).astype(g_ref.dtype)

    @pl.when(tci == pl.num_programs(2) - 1)
    def _():
        ht_ref[...] = st[...]


def _s5_scan(u_tm, bblk, cblk, a, d_skip, h0_blocks, tc_len):
    nbh, rows, d = u_tm.shape
    nb = d // _LANES
    length = rows // _BSUB
    ntc = length // tc_len
    tr = tc_len * _BSUB
    nstate = 2 * _GB * _SSM_P
    return pl.pallas_call(
        functools.partial(_s5_scan_kernel, tc_len),
        grid=(nbh, nb, ntc),
        in_specs=[
            pl.BlockSpec((None, tr, _LANES), lambda h, b, t: (h, t, b)),
            pl.BlockSpec((None, _LANES, nstate), lambda h, b, t: (b, 0, 0)),
            pl.BlockSpec((None, nstate, _LANES), lambda h, b, t: (b, 0, 0)),
            pl.BlockSpec((None, _BSUB, nstate), lambda h, b, t: (b, 0, 0)),
            pl.BlockSpec((1, _LANES), lambda h, b, t: (0, b)),
            pl.BlockSpec((None, None, _BSUB, nstate), lambda h, b, t: (h, b, 0, 0)),
        ],
        out_specs=[
            pl.BlockSpec((None, tr, _LANES), lambda h, b, t: (h, t, b)),
            pl.BlockSpec((None, None, _BSUB, nstate), lambda h, b, t: (h, b, 0, 0)),
        ],
        out_shape=[jax.ShapeDtypeStruct((nbh, rows, d), _MXU_DTYPE),
                   jax.ShapeDtypeStruct((nbh, nb, _BSUB, nstate), _F32)],
        scratch_shapes=[pltpu.VMEM((tr, nstate), _F32), pltpu.VMEM((_BSUB, nstate), _F32)],
        compiler_params=_params("arbitrary", "arbitrary", "arbitrary"),
    )(u_tm, bblk, cblk, a, d_skip.reshape(1, d), h0_blocks)


def _route(xn, wr_ref, first_tile, carry_ref, route_ref, w1b_ref, w2b_ref, cnt_ref):
    tm = xn.shape[0]
    logits = _dot_hi(xn, wr_ref[...])
    lane = lax.broadcasted_iota(jnp.int32, logits.shape, 1).astype(_F32)
    neg, big = -jnp.inf, float(_LANES)
    lg = jnp.where(lane < _NG, logits, neg)
    g_top = jnp.max(lg, -1, keepdims=True)
    g_sel = jnp.min(jnp.where(lg == g_top, lane, big), -1, keepdims=True)
    lse = jnp.log(jnp.sum(jnp.exp(lg - g_top), -1, keepdims=True)) + g_top
    g_w = jnp.exp(g_top - lse)
    lo = _NG + g_sel * _EPG
    le = jnp.where((lane >= lo) & (lane < lo + _EPG), logits, neg)
    m1 = jnp.max(le, -1, keepdims=True)
    i1 = jnp.min(jnp.where(le == m1, lane, big), -1, keepdims=True)
    le2 = jnp.where(lane == i1, neg, le)
    m2 = jnp.max(le2, -1, keepdims=True)
    i2 = jnp.min(jnp.where(le2 == m2, lane, big), -1, keepdims=True)
    ex = jnp.exp(m2 - m1)
    w1 = g_w * (1.0 / (1.0 + ex))
    w2 = g_w * (ex / (1.0 + ex))

    @pl.when(first_tile)
    def _():
        carry_ref[...] = jnp.zeros_like(carry_ref)

    hit1, hit2 = lane == i1, lane == i2
    onehot = jnp.where(hit1 | hit2, 1.0, 0.0)
    r = lax.broadcasted_iota(jnp.int32, (tm, tm), 0)
    c = lax.broadcasted_iota(jnp.int32, (tm, tm), 1)
    tri = jnp.where(c < r, 1.0, 0.0)
    pref = _dot(tri, onehot) + carry_ref[...]
    rank1 = jnp.sum(jnp.where(hit1, pref, 0.0), -1, keepdims=True)
    rank2 = jnp.sum(jnp.where(hit2, pref, 0.0), -1, keepdims=True)
    carry = carry_ref[...] + jnp.sum(onehot, 0, keepdims=True)
    carry_ref[...] = carry
    cnt_ref[...] = jnp.broadcast_to(carry, cnt_ref.shape)
    slab = jnp.where(lane == 0, i1 - _NG, jnp.where(lane == 1, i2 - _NG,
                     jnp.where(lane == 2, rank1, jnp.where(lane == 3, rank2, 0.0))))
    route_ref[...] = slab[:, :_SUBLANES]
    w1b_ref[...] = jnp.broadcast_to(w1, w1b_ref.shape)
    w2b_ref[...] = jnp.broadcast_to(w2, w2b_ref.shape)


def _route_specs(tm, tpc, tile_of):
    def tile_map(*ids):
        return (tile_of(*ids), 0)

    def chunk_map(*ids):
        return (tile_of(*ids) // tpc, 0, 0)

    specs = [pl.BlockSpec((tm, _SUBLANES), tile_map), pl.BlockSpec((tm, _LANES), tile_map),
             pl.BlockSpec((tm, _LANES), tile_map), pl.BlockSpec((None, _SUBLANES, _LANES), chunk_map)]
    return specs


def _route_shapes(t, n_chunks):
    return [jax.ShapeDtypeStruct((t, _SUBLANES), _F32), jax.ShapeDtypeStruct((t, _LANES), _F32),
            jax.ShapeDtypeStruct((t, _LANES), _F32), jax.ShapeDtypeStruct((n_chunks, _SUBLANES, _LANES), _F32)]


def _router_weight(w_group, w_expert):
    d = w_group.shape[0]
    pad = jnp.zeros((d, _LANES - _NG - _NE), _F32)
    return jnp.concatenate([w_group, w_expert, pad], axis=1)


def _mix_ln_route_kernel(glu, tpc, f_ref, x_ref, w_ref, lng_ref, lnb_ref, wr_ref,
                         o_ref, route_ref, w1b_ref, w2b_ref, cnt_ref, carry_ref):
    i = pl.program_id(0)
    h = _dot(f_ref[...], w_ref[...])
    if glu:
        d = h.shape[1] // 2
        h = h[:, :d] * jax.nn.sigmoid(h[:, d:])
    xn = _layer_norm(_ALPHA * x_ref[...] + h, lng_ref[...], lnb_ref[...])
    o_ref[...] = xn
    _route(xn, wr_ref, i % tpc == 0, carry_ref, route_ref, w1b_ref, w2b_ref, cnt_ref)


def _mix_ln_route(f, f_map, f_shape_block, x, w, ln_g, ln_b, wr, tm, tk, glu):
    t, d = x.shape
    tpc = tk // tm
    n_chunks = t // tk
    kin, n = w.shape
    row = lambda a: a.reshape(1, d)
    const = lambda i: (0, 0)
    return pl.pallas_call(
        functools.partial(_mix_ln_route_kernel, glu, tpc),
        grid=(t // tm,),
        in_specs=[pl.BlockSpec(f_shape_block, f_map), pl.BlockSpec((tm, d), lambda i: (i, 0)),
                  pl.BlockSpec((kin, n), const), pl.BlockSpec((1, d), const), pl.BlockSpec((1, d), const),
                  pl.BlockSpec((d, _LANES), const)],
        out_specs=[pl.BlockSpec((tm, d), lambda i: (i, 0))] + _route_specs(tm, tpc, lambda i: i),
        out_shape=[jax.ShapeDtypeStruct((t, d), _F32)] + _route_shapes(t, n_chunks),
        scratch_shapes=[pltpu.VMEM((1, _LANES), _F32)],
        compiler_params=_params("arbitrary"),
    )(f, x, _mx(w), row(ln_g), row(ln_b), wr)


def _moe_plan(route, cnt, tk):
    t = route.shape[0]
    n_chunks = t // tk
    ri = route.astype(jnp.int32)
    e1, e2, r1, r2 = (ri[:, k].reshape(n_chunks, tk) for k in range(4))
    counts = cnt[:, 0, _NG:_NG + _NE].astype(jnp.int32)
    padded = (counts + _SUBLANES - 1) // _SUBLANES * _SUBLANES
    offs = jnp.cumsum(padded, axis=1) - padded
    d1 = jnp.take_along_axis(offs, e1, axis=1) + r1
    d2 = jnp.take_along_axis(offs, e2, axis=1) + r2
    dest = (d1 + (d2 << 16)).reshape(t)
    return dest, offs.reshape(-1), counts.reshape(-1)


def _moe_kernel(tk, dest_ref, offs_ref, cnts_ref, x_ref, w1b_ref, w2b_ref, wgu_ref, wd_ref, lng_ref, lnb_ref,
                o_ref, ys_ref, gath_ref, inv_ref):
    c, e = pl.program_id(0), pl.program_id(1)
    n_slots = inv_ref.shape[0]
    d = x_ref.shape[1]

    @pl.when(e == 0)
    def _():
        def clear(i, _):
            inv_ref[i] = 0
            return 0
        lax.fori_loop(0, n_slots, clear, 0)

        def fill(t, _):
            dd = dest_ref[c * tk + t]
            inv_ref[dd & 0xFFFF] = t
            inv_ref[dd >> 16] = t
            return 0
        lax.fori_loop(0, tk, fill, 0)

    n = cnts_ref[c * _NE + e]
    off = offs_ref[c * _NE + e]

    def block(j, _):
        base = pl.multiple_of(off + j * _MOE_ROWS, _SUBLANES)

        def gather(r, _):
            tok = inv_ref[base + r]
            gath_ref[pl.ds(r, 1), :] = x_ref[pl.ds(tok, 1), :]
            return 0
        lax.fori_loop(0, _MOE_ROWS, gather, 0)
        h = _dot(gath_ref[...], wgu_ref[...])
        act = jax.nn.silu(h[:, :_D_EXPERT]) * h[:, _D_EXPERT:]
        ys_ref[pl.ds(base, _MOE_ROWS), :] = _dot(act, wd_ref[...])
        return 0
    lax.fori_loop(0, (n + _MOE_ROWS - 1) // _MOE_ROWS, block, 0)

    @pl.when(e == _NE - 1)
    def _():
        reps = d // _LANES

        def combine(t, _):
            dd = dest_ref[c * tk + t]
            y1 = ys_ref[pl.ds(dd & 0xFFFF, 1), :]
            y2 = ys_ref[pl.ds(dd >> 16, 1), :]
            w1 = jnp.tile(w1b_ref[pl.ds(t, 1), :], (1, reps))
            w2 = jnp.tile(w2b_ref[pl.ds(t, 1), :], (1, reps))
            o_ref[pl.ds(t, 1), :] = _ALPHA * x_ref[pl.ds(t, 1), :] + (y1 * w1 + y2 * w2)
            return 0
        lax.fori_loop(0, tk, combine, 0)
        rt = min(tk, 256)
        for r0 in range(0, tk, rt):
            o_ref[r0:r0 + rt, :] = _layer_norm(o_ref[r0:r0 + rt, :], lng_ref[...], lnb_ref[...])


def _moe(x, route, w1b, w2b, cnt, w_gate_up, w_down, ln_g, ln_b, tk):
    t, d = x.shape
    n_chunks = t // tk
    dest, offs, cnts = _moe_plan(route, cnt, tk)
    n_slots = 2 * tk + _NE * _SUBLANES + _MOE_ROWS
    chunk = lambda c, e, *_: (c, 0)
    const = lambda c, e, *_: (0, 0)
    expert = lambda c, e, *_: (e, 0, 0)
    grid_spec = pltpu.PrefetchScalarGridSpec(
        num_scalar_prefetch=3,
        grid=(n_chunks, _NE),
        in_specs=[pl.BlockSpec((tk, d), chunk, pipeline_mode=pl.Buffered(1)),
                  pl.BlockSpec((tk, _LANES), chunk, pipeline_mode=pl.Buffered(1)),
                  pl.BlockSpec((tk, _LANES), chunk, pipeline_mode=pl.Buffered(1)),
                  pl.BlockSpec((None, d, 2 * _D_EXPERT), expert), pl.BlockSpec((None, _D_EXPERT, d), expert),
                  pl.BlockSpec((1, d), const), pl.BlockSpec((1, d), const)],
        out_specs=pl.BlockSpec((tk, d), chunk),
        scratch_shapes=[pltpu.VMEM((n_slots, d), _F32), pltpu.VMEM((_MOE_ROWS, d), _F32),
                        pltpu.SMEM((n_slots,), jnp.int32)],
    )
    return pl.pallas_call(
        functools.partial(_moe_kernel, tk),
        grid_spec=grid_spec,
        out_shape=jax.ShapeDtypeStruct((t, d), _F32),
        compiler_params=_params("arbitrary", "arbitrary"),
    )(dest, offs, cnts, x, w1b, w2b, _mx(w_gate_up), _mx(w_down), ln_g.reshape(1, d), ln_b.reshape(1, d))


def _layer0(x, h0, s5w, w_in, d_skip, w_glu, ln_g, ln_b, moe_w, tm, tk, tc_len):
    b, length, d = x.shape
    nbh = b // _BSUB
    bblk, cblk, a = s5w
    x2 = x.reshape(b * length, d)
    if length % tm == 0:
        nt = length // tm
        u_tm = _proj(x2, w_in, tm, [(0, d)], [_F32], grid=(b, nt), x_map=lambda s, j: (s * nt + j, 0),
                     out_shapes=[(nbh * length, _BSUB * d)],
                     out_maps=[lambda s, j: ((s // _BSUB) * nt + j, s % _BSUB)])[0]
        u_tm = u_tm.reshape(nbh, length * _BSUB, d)
    else:
        u = _proj(x2, w_in, b * length, [(0, d)], [_F32])[0]
        u_tm = jnp.transpose(u.reshape(nbh, _BSUB, length, d), (0, 2, 1, 3)).reshape(nbh, length * _BSUB, d)
    g_tm, h_t = _s5_scan(u_tm, bblk, cblk, a, d_skip, _state_to_blocks(h0), tc_len)
    if length % tm == 0:
        nt = length // tm
        f = g_tm.reshape(nbh * length, _BSUB * d)
        f_map = lambda i: (((i // nt) // _BSUB) * nt + i % nt, (i // nt) % _BSUB)
    else:
        f = jnp.transpose(g_tm.reshape(nbh, length, _BSUB, d), (0, 2, 1, 3)).reshape(b * length, d)
        f_map = lambda i: (i, 0)
    w_group, w_expert, w_gate_up, w_down = moe_w
    wr = _router_weight(w_group, w_expert)
    x1, route, w1b, w2b, cnt = _mix_ln_route(f, f_map, (tm, d), x2, w_glu, ln_g[0], ln_b[0], wr, tm, tk, True)
    x2o = _moe(x1, route, w1b, w2b, cnt, w_gate_up, w_down, ln_g[1], ln_b[1], tk)
    return x2o, _blocks_to_state(h_t)


_HALF = _LANES // 2
_N_PAIR = _N_KV // 2
_KV_HALF = _N_KV * _HEAD_DIM


def _head_perm():
    idx = []
    for p in range(_N_PAIR):
        for h in range(_HPG):
            for gs in range(2):
                base = ((2 * p + gs) * _HPG + h) * _HEAD_DIM
                idx.extend(range(base, base + _HEAD_DIM))
    return np.asarray(idx, np.int32)


def _gate_col(p, h, gs, branch):
    return ((2 * p + gs) * _HPG + h) * _N_BRANCH + branch


def _nsa_weights(kv_w, cmp_pe, cmp_w1, cmp_w2, w_qg, w_o):
    nq = _N_HEADS * _HEAD_DIM
    perm = _head_perm()
    w_q = w_qg[:, :nq][:, perm] * _ATTN_SCALE
    w_g = w_qg[:, nq:]
    w_g = jnp.pad(w_g, ((0, 0), (0, _LANES - w_g.shape[1])))
    half = _CMP_STRIDE * _HEAD_DIM
    w1 = jnp.concatenate([cmp_w1[:, :half], cmp_w1[:, half:]], axis=2)
    pe = cmp_pe.reshape(2, 2, 1, half)
    pe = jnp.broadcast_to(pe, (2, 2, _SUBLANES, half))
    return dict(kv_w=kv_w, w_qg=jnp.concatenate([w_q, w_g], axis=1), w_o=w_o[perm], w1=_mx(w1), w2=_mx(cmp_w2), pe=pe)


def _sel_sum_matrix(n_rows, n_cmp, n_sel, lanes):
    m = np.zeros((n_rows, lanes), np.float32)
    for c in range(n_cmp):
        j0 = (c * _CMP_STRIDE) // _SEL_BLOCK
        j1 = (c * _CMP_STRIDE + _CMP_BLOCK - 1) // _SEL_BLOCK
        if j0 < n_sel:
            m[c, j0] += 1.0
        if j1 != j0 and j1 < n_sel:
            m[c, j1] += 1.0
    return jnp.asarray(m, _MXU_DTYPE)


def _expand_matrix(lanes, n_keys):
    j = np.arange(lanes)[:, None]
    key = np.arange(n_keys)[None, :]
    return jnp.asarray((key // _SEL_BLOCK == j).astype(np.float32), _MXU_DTYPE)


def _compress_kernel(npg, pt_ref, *refs):
    page_refs = refs[:npg]
    halo_ref, w1_ref, w2_ref, pe_ref, o_ref, lhs_ref = refs[npg:]
    nlb = 2 * _N_PAIR
    spp = page_refs[0].shape[0] // (nlb * _CMP_STRIDE)
    seg = npg * spp + _SUBLANES
    rows = _N_KV * seg
    hd = _HEAD_DIM

    @pl.when((pl.program_id(0) == 0) & (pl.program_id(1) == 0))
    def _():
        lhs_ref[...] = jnp.zeros_like(lhs_ref)

    def scatter(xs, s, c, r0, nrow):
        for half in range(2):
            k, g = divmod(2 * c + half, _N_KV)
            lhs_ref[k, g * seg + r0:g * seg + r0 + nrow, s * hd:(s + 1) * hd] = xs[:, half * hd:(half + 1) * hd]

    for p in range(npg):
        for s in range(_CMP_STRIDE):
            for c in range(nlb):
                scatter(page_refs[p][pl.ds(s * nlb + c, spp, stride=nlb * _CMP_STRIDE), :], s, c, p * spp, spp)
    for s in range(_CMP_STRIDE):
        for c in range(nlb):
            scatter(halo_ref[s * nlb + c:s * nlb + c + 1, :], s, c, npg * spp, 1)

    pieces = []
    for k in range(2):
        w_lo, w_hi = w1_ref[k, :, :_CMP_HIDDEN], w1_ref[k, :, _CMP_HIDDEN:]
        lo = _dot(lhs_ref[k, 0:rows, :], w_lo)
        hi = _dot(lhs_ref[k, 1:rows + 1, :], w_hi)
        pe = _dot(pe_ref[k, 0], w_lo)[0:1] + _dot(pe_ref[k, 1], w_hi)[0:1]
        out = _dot(jax.nn.gelu(lo + hi + pe), w2_ref[k])
        pieces += [out[g * seg:g * seg + npg * spp] for g in range(_N_KV)]
    o_ref[...] = jnp.concatenate(pieces, axis=1).astype(o_ref.dtype)


def _compress(pages, table, nsa, npg):
    b, npp = table.shape
    n_pages, page, width = pages.shape
    nlb = width // _LANES
    pages = pages.reshape(n_pages, page * nlb, _LANES)
    spp = page // _CMP_STRIDE
    seg = npg * spp + _SUBLANES
    half = _CMP_STRIDE * _HEAD_DIM

    def page_map(i):
        return lambda s, j, pt: (pt[s * npp + j * npg + i], 0, 0)

    def halo_map(s, j, pt):
        return (pt[s * npp + jnp.minimum((j + 1) * npg, npp - 1)], 0, 0)

    const3 = lambda s, j, pt: (0, 0, 0)
    const4 = lambda s, j, pt: (0, 0, 0, 0)
    grid_spec = pltpu.PrefetchScalarGridSpec(
        num_scalar_prefetch=1,
        grid=(b, npp // npg),
        in_specs=[pl.BlockSpec((None, page * nlb, _LANES), page_map(i)) for i in range(npg)]
        + [pl.BlockSpec((None, _CMP_STRIDE * nlb, _LANES), halo_map),
           pl.BlockSpec((2, half, 2 * _CMP_HIDDEN), const3), pl.BlockSpec((2, _CMP_HIDDEN, _HEAD_DIM), const3),
           pl.BlockSpec((2, 2, _SUBLANES, half), const4)],
        out_specs=pl.BlockSpec((None, npg * spp, width), lambda s, j, pt: (s, j, 0)),
        scratch_shapes=[pltpu.VMEM((2, _N_KV * seg + _SUBLANES, half), _F32)],
    )
    return pl.pallas_call(
        functools.partial(_compress_kernel, npg),
        grid_spec=grid_spec,
        out_shape=jax.ShapeDtypeStruct((b, npp * spp, width), _MXU_DTYPE),
        compiler_params=_params("arbitrary", "arbitrary"),
    )(table.reshape(-1), *([pages] * npg), pages, nsa["w1"], nsa["w2"], nsa["pe"])


def _softmax_parts(s, mask):
    sm = jnp.where(mask, s, _NEG_INF)
    m = jnp.max(sm, -1, keepdims=True)
    e = jnp.where(mask, jnp.exp(sm - m), 0.0)
    return e, jnp.maximum(jnp.sum(e, -1, keepdims=True), _TINY)


def _dot_exact_rhs(a, b):
    a1 = _mx(a)
    r1 = a - a1.astype(_F32)
    a2 = _mx(r1)
    a3 = _mx(r1 - a2.astype(_F32))
    return _dot(a1, b) + (_dot(a2, b) + _dot(a3, b))


def _top_k_mask(score, k):
    lane = lax.broadcasted_iota(jnp.int32, score.shape, 1).astype(_F32)
    sel = jnp.zeros(score.shape, _F32)
    for _ in range(k):
        m = jnp.max(score, -1, keepdims=True)
        idx = jnp.min(jnp.where(score == m, lane, float(score.shape[1])), -1, keepdims=True)
        hit = lane == idx
        sel = jnp.where(hit, 1.0, sel)
        score = jnp.where(hit, -jnp.inf, score)
    return sel


def _select_mask(pg, mt_ref, q_pos, n_sel, k_sel):
    ps = _dot_exact_rhs(pg, mt_ref[...])
    j = lax.broadcasted_iota(jnp.int32, ps.shape, 1)
    valid = j * _SEL_BLOCK <= q_pos
    own = jnp.right_shift(q_pos, int(math.log2(_SEL_BLOCK)))
    forced = jnp.where(j == own, 1.0, jnp.where(j == 0, 1.0, 0.0))
    score = jnp.where(valid, ps + _FORCE_SCORE * forced, -1.0)
    score = jnp.where(j < n_sel, score, -jnp.inf)
    return _top_k_mask(score, k_sel)


def _masked_halves(blk):
    low = lax.broadcasted_iota(jnp.int32, blk.shape, 1) < _HALF
    return jnp.concatenate([jnp.where(low, blk, 0.0), jnp.where(low, 0.0, blk)], axis=0)


def _merge_halves(o, rows):
    low = lax.broadcasted_iota(jnp.int32, (rows, _LANES), 1) < _HALF
    return jnp.where(low, o[:rows], o[rows:])


def _gate_block(gates, p, h, branch):
    low = lax.broadcasted_iota(jnp.int32, (gates.shape[0], _LANES), 1) < _HALF
    c0, c1 = _gate_col(p, h, 0, branch), _gate_col(p, h, 1, branch)
    return jnp.where(low, gates[:, c0:c0 + 1], gates[:, c1:c1 + 1])


def _attn_prompt_kernel(tq, n_sel, k_sel, q_ref, gate_ref, ckv_ref, slc_ref, win_ref, mt_ref, ex_ref, o_ref):
    t0 = pl.program_id(1) * tq
    length = slc_ref.shape[0]
    nc = ckv_ref.shape[0]
    q_pos = t0 + lax.broadcasted_iota(jnp.int32, (tq, 1), 0)
    qf = q_ref[...].astype(_F32)
    gates = gate_ref[...]
    c_end = lax.broadcasted_iota(jnp.int32, (1, nc), 1) * _CMP_STRIDE + (_CMP_BLOCK - 1)
    cmask = c_end <= q_pos
    cmask2 = jnp.concatenate([cmask, cmask], axis=0)
    causal = lax.broadcasted_iota(jnp.int32, (1, length), 1) <= q_pos
    wk = min(length, tq + _WINDOW)
    wstart = pl.multiple_of(jnp.clip(t0 - _WINDOW, 0, length - wk), tq)
    dist = q_pos - (wstart + lax.broadcasted_iota(jnp.int32, (1, wk), 1))
    wmask = (dist >= 0) & (dist < _WINDOW)
    wmask2 = jnp.concatenate([wmask, wmask], axis=0)

    for p in range(_N_PAIR):
        ksl = slice(p * _LANES, (p + 1) * _LANES)
        vsl = slice(_KV_HALF + p * _LANES, _KV_HALF + (p + 1) * _LANES)
        kc, vc = ckv_ref[:, ksl], ckv_ref[:, vsl]
        lhs, o_cmp = [], []
        pg = [jnp.zeros((tq, nc), _F32), jnp.zeros((tq, nc), _F32)]
        for h in range(_HPG):
            blk = p * _HPG + h
            lq = _mx(_masked_halves(qf[:, blk * _LANES:(blk + 1) * _LANES]))
            lhs.append(lq)
            e, den = _softmax_parts(_dot_nt(lq, kc), cmask2)
            pc = e / den
            o_cmp.append(_merge_halves(_dot(pc, vc), tq))
            pg = [pg[0] + pc[:tq], pg[1] + pc[tq:]]
        smask = []
        for gs in range(2):
            sel = _select_mask(pg[gs], mt_ref, q_pos, n_sel, k_sel)
            smask.append((_dot(sel, ex_ref[...]) > 0.5) & causal)
        smask2 = jnp.concatenate(smask, axis=0)
        ks, vs = slc_ref[:, ksl], slc_ref[:, vsl]
        kw, vw = win_ref[pl.ds(wstart, wk), ksl], win_ref[pl.ds(wstart, wk), vsl]
        for h in range(_HPG):
            blk = p * _HPG + h
            e, den = _softmax_parts(_dot_nt(lhs[h], ks), smask2)
            o_slc = _merge_halves(_dot(e, vs) / den, tq)
            e, den = _softmax_parts(_dot_nt(lhs[h], kw), wmask2)
            o_win = _merge_halves(_dot(e, vw) / den, tq)
            o = (_gate_block(gates, p, h, 0) * o_cmp[h] + _gate_block(gates, p, h, 1) * o_slc
                 + _gate_block(gates, p, h, 2) * o_win)
            o_ref[:, blk * _LANES:(blk + 1) * _LANES] = o.astype(o_ref.dtype)


def _nsa_prompt(x, b, length, nsa, tm):
    t, d = x.shape
    kvw = 2 * _KV_HALF
    cmp_f, slc_f, win_f, slc_b, win_b = _proj(
        x, nsa["kv_w"], tm, [(0, kvw), (kvw, 2 * kvw), (2 * kvw, 3 * kvw), (kvw, 2 * kvw), (2 * kvw, 3 * kvw)],
        [_F32, _F32, _F32, _MXU_DTYPE, _MXU_DTYPE])
    nq = _N_HEADS * _HEAD_DIM
    q, gates = _proj(x, nsa["w_qg"], tm, [(0, nq), (nq, nq + _LANES)], [_MXU_DTYPE, _F32], acts=[None, "sigmoid"])
    page = _LANES
    npp = length // page
    table = jnp.arange(b * npp, dtype=jnp.int32).reshape(b, npp)
    ckv = _compress(cmp_f.reshape(b * npp, page, kvw), table, nsa, min(8, npp))
    nc = length // _CMP_STRIDE
    n_sel = -(-length // _SEL_BLOCK)
    tq = _LANES
    nq_t = length // tq
    mt = _sel_sum_matrix(nc, nc - 1, n_sel, _LANES)
    ex = _expand_matrix(_LANES, length)
    seq = lambda s, i: (s, 0, 0)
    const = lambda s, i: (0, 0)
    o = pl.pallas_call(
        functools.partial(_attn_prompt_kernel, tq, n_sel, min(_N_SELECT, n_sel)),
        grid=(b, nq_t),
        in_specs=[pl.BlockSpec((tq, nq), lambda s, i: (s * nq_t + i, 0)),
                  pl.BlockSpec((tq, _LANES), lambda s, i: (s * nq_t + i, 0)),
                  pl.BlockSpec((None, nc, kvw), seq), pl.BlockSpec((None, length, kvw), seq),
                  pl.BlockSpec((None, length, kvw), seq),
                  pl.BlockSpec((nc, _LANES), const), pl.BlockSpec((_LANES, length), const)],
        out_specs=pl.BlockSpec((tq, nq), lambda s, i: (s * nq_t + i, 0)),
        out_shape=jax.ShapeDtypeStruct((t, nq), _MXU_DTYPE),
        compiler_params=_params("arbitrary", "arbitrary"),
    )(q, gates, ckv, slc_b.reshape(b, length, kvw), win_b.reshape(b, length, kvw), mt, ex)
    return o, cmp_f, slc_f, win_f


def _attn_sample_cmp_kernel(n_sel, k_sel, past, q_ref, ckv_ref, mt_ref, ocmp_ref, sel_ref):
    s_new = q_ref.shape[0]
    nc = ckv_ref.shape[0]
    q_pos = past + lax.broadcasted_iota(jnp.int32, (s_new, 1), 0)
    qf = q_ref[...].astype(_F32)
    c_end = lax.broadcasted_iota(jnp.int32, (1, nc), 1) * _CMP_STRIDE + (_CMP_BLOCK - 1)
    cmask = c_end <= q_pos
    cmask2 = jnp.concatenate([cmask, cmask], axis=0)
    for p in range(_N_PAIR):
        kc = ckv_ref[:, p * _LANES:(p + 1) * _LANES]
        vc = ckv_ref[:, _KV_HALF + p * _LANES:_KV_HALF + (p + 1) * _LANES]
        pg = [jnp.zeros((s_new, nc), _F32), jnp.zeros((s_new, nc), _F32)]
        for h in range(_HPG):
            blk = p * _HPG + h
            lq = _masked_halves(qf[:, blk * _LANES:(blk + 1) * _LANES])
            e, den = _softmax_parts(_dot_nt(lq, kc), cmask2)
            pc = e / den
            ocmp_ref[:, blk * _LANES:(blk + 1) * _LANES] = _merge_halves(_dot(pc, vc), s_new)
            pg = [pg[0] + pc[:s_new], pg[1] + pc[s_new:]]
        for gs in range(2):
            sel_ref[2 * p + gs] = _select_mask(pg[gs], mt_ref, q_pos, n_sel, k_sel)


def _attn_sample_kernel(npg, past, pt_ref, *refs):
    page_refs = refs[:npg]
    (q_ref, gate_ref, ocmp_ref, selc_ref, seln_ref, slcn_ref, winc_ref, winn_ref, ex_ref,
     o_ref, m_ref, l_ref, acc_ref) = refs[npg:]
    j = pl.program_id(1)
    s_new = q_ref.shape[0]
    rows = 2 * _HPG * s_new
    qf = q_ref[...].astype(_F32)

    @pl.when(j == 0)
    def _():
        m_ref[...] = jnp.full(m_ref.shape, _NEG_INF, _F32)
        l_ref[...] = jnp.zeros_like(l_ref)
        acc_ref[...] = jnp.zeros_like(acc_ref)

    def pair_lhs(p):
        halves = [_masked_halves(qf[:, (p * _HPG + h) * _LANES:(p * _HPG + h + 1) * _LANES]) for h in range(_HPG)]
        return jnp.concatenate([hv[:s_new] for hv in halves] + [hv[s_new:] for hv in halves], axis=0)

    def online(p, s, mask, v):
        sm = jnp.where(mask, s, _NEG_INF)
        m_old = m_ref[p]
        m_new = jnp.maximum(m_old, jnp.max(sm, -1, keepdims=True))
        alpha = jnp.exp(m_old - m_new)
        e = jnp.where(mask, jnp.exp(sm - m_new), 0.0)
        l_ref[p] = alpha * l_ref[p] + jnp.sum(e, -1, keepdims=True)
        acc_ref[p] = alpha * acc_ref[p] + _dot(e, v)
        m_ref[p] = m_new

    for p in range(_N_PAIR):
        ksl = slice(p * _LANES, (p + 1) * _LANES)
        vsl = slice(_KV_HALF + p * _LANES, _KV_HALF + (p + 1) * _LANES)
        lq = pair_lhs(p)
        k = jnp.concatenate([r[:, ksl] for r in page_refs], axis=0)
        v = jnp.concatenate([r[:, vsl] for r in page_refs], axis=0)
        mask = jnp.concatenate(
            [jnp.tile(_dot(selc_ref[2 * p + gs], ex_ref[...]) > 0.5, (_HPG, 1)) for gs in range(2)], axis=0)
        online(p, _dot_nt(lq, k), mask, v)

    @pl.when(j == pl.num_programs(1) - 1)
    def _():
        gates = gate_ref[...]
        tok = lax.broadcasted_iota(jnp.int32, (s_new, 1), 0)
        q_pos = past + jnp.tile(tok, (2 * _HPG, 1))
        newer = lax.broadcasted_iota(jnp.int32, (1, s_new), 1) <= jnp.tile(tok, (2 * _HPG, 1))
        n_cache = winc_ref.shape[0]
        k_pos = past - n_cache + lax.broadcasted_iota(jnp.int32, (1, n_cache + s_new), 1)
        dist = q_pos - k_pos
        wmask = (dist >= 0) & (dist < _WINDOW) & (k_pos >= 0)
        for p in range(_N_PAIR):
            ksl = slice(p * _LANES, (p + 1) * _LANES)
            vsl = slice(_KV_HALF + p * _LANES, _KV_HALF + (p + 1) * _LANES)
            lq = pair_lhs(p)
            nmask = jnp.concatenate(
                [jnp.tile(seln_ref[2 * p + gs][:, :s_new] > 0.5, (_HPG, 1)) for gs in range(2)], axis=0) & newer
            online(p, _dot_nt(lq, slcn_ref[:, ksl]), nmask, slcn_ref[:, vsl])
            o_slc = acc_ref[p] / jnp.maximum(l_ref[p], _TINY)
            kw = jnp.concatenate([winc_ref[:, ksl], winn_ref[:, ksl]], axis=0)
            vw = jnp.concatenate([winc_ref[:, vsl], winn_ref[:, vsl]], axis=0)
            e, den = _softmax_parts(_dot_nt(lq, kw), wmask)
            o_win = _dot(e, vw) / den
            for h in range(_HPG):
                blk = p * _HPG + h
                lo, hi = h * s_new, (_HPG + h) * s_new
                pick = lambda o: _merge_halves(jnp.concatenate([o[lo:lo + s_new], o[hi:hi + s_new]], axis=0), s_new)
                o = (_gate_block(gates, p, h, 0) * ocmp_ref[:, blk * _LANES:(blk + 1) * _LANES]
                     + _gate_block(gates, p, h, 1) * pick(o_slc) + _gate_block(gates, p, h, 2) * pick(o_win))
                o_ref[:, blk * _LANES:(blk + 1) * _LANES] = o


def _nsa_sample(x, b, s_new, nsa, cache_cmp_kv, cache_slc_kv, cache_win_kv, page_table):
    t, d = x.shape
    kvw = 2 * _KV_HALF
    nq = _N_HEADS * _HEAD_DIM
    n_pool, page = cache_cmp_kv.shape[:2]
    npp = page_table.shape[1]
    past = npp * page
    assert (past + s_new) // _CMP_STRIDE == past // _CMP_STRIDE and past % _SEL_BLOCK == 0
    cmp_f, slc_f, win_f = _proj(x, nsa["kv_w"], t, [(0, kvw), (kvw, 2 * kvw), (2 * kvw, 3 * kvw)], [_F32] * 3)
    q, gates = _proj(x, nsa["w_qg"], t, [(0, nq), (nq, nq + _LANES)], [_F32, _F32], acts=[None, "sigmoid"])
    ckv = _compress(cache_cmp_kv.reshape(n_pool, page, kvw), page_table, nsa, min(8, npp))
    nc = past // _CMP_STRIDE
    n_sel = -(-(past + s_new) // _SEL_BLOCK)
    sel_lanes = -(-n_sel // _LANES) * _LANES
    mt = _sel_sum_matrix(nc, nc - 1, n_sel, sel_lanes)
    seq2 = lambda s: (s, 0)
    o_cmp, sel = pl.pallas_call(
        functools.partial(_attn_sample_cmp_kernel, n_sel, min(_N_SELECT, n_sel), past),
        grid=(b,),
        in_specs=[pl.BlockSpec((s_new, nq), seq2), pl.BlockSpec((None, nc, kvw), lambda s: (s, 0, 0)),
                  pl.BlockSpec((nc, sel_lanes), lambda s: (0, 0))],
        out_specs=[pl.BlockSpec((s_new, nq), seq2),
                   pl.BlockSpec((None, _N_KV, s_new, sel_lanes), lambda s: (s, 0, 0, 0))],
        out_shape=[jax.ShapeDtypeStruct((t, nq), _F32), jax.ShapeDtypeStruct((b, _N_KV, s_new, sel_lanes), _F32)],
        compiler_params=_params("arbitrary"),
    )(q, ckv, mt)

    npg = min(16, npp)
    n_chunks = npp // npg
    bpc = npg * page // _SEL_BLOCK
    n_past_blocks = past // _SEL_BLOCK
    selc = sel[..., :n_past_blocks].reshape(b, _N_KV, s_new, n_chunks, bpc)
    selc = jnp.pad(jnp.transpose(selc, (0, 3, 1, 2, 4)), ((0, 0),) * 4 + ((0, _LANES - bpc),))
    seln = jnp.broadcast_to(sel[..., n_past_blocks:n_past_blocks + 1], (b, _N_KV, s_new, _LANES))
    ex = _expand_matrix(_LANES, npg * page)
    n_cache = cache_win_kv.shape[1]

    def page_map(i):
        return lambda s, j, pt: (pt[s * npp + j * npg + i], 0, 0)

    seq = lambda s, j, pt: (s, 0)
    seq3 = lambda s, j, pt: (s, 0, 0)
    grid_spec = pltpu.PrefetchScalarGridSpec(
        num_scalar_prefetch=1,
        grid=(b, n_chunks),
        in_specs=[pl.BlockSpec((None, page, kvw), page_map(i)) for i in range(npg)]
        + [pl.BlockSpec((s_new, nq), seq), pl.BlockSpec((s_new, _LANES), seq), pl.BlockSpec((s_new, nq), seq),
           pl.BlockSpec((None, None, _N_KV, s_new, _LANES), lambda s, j, pt: (s, j, 0, 0, 0)),
           pl.BlockSpec((None, _N_KV, s_new, _LANES), lambda s, j, pt: (s, 0, 0, 0)),
           pl.BlockSpec((s_new, kvw), seq), pl.BlockSpec((None, n_cache, kvw), seq3), pl.BlockSpec((s_new, kvw), seq),
           pl.BlockSpec((_LANES, npg * page), lambda s, j, pt: (0, 0))],
        out_specs=pl.BlockSpec((s_new, nq), seq),
        scratch_shapes=[pltpu.VMEM((_N_PAIR, 2 * _HPG * s_new, 1), _F32), pltpu.VMEM((_N_PAIR, 2 * _HPG * s_new, 1), _F32),
                        pltpu.VMEM((_N_PAIR, 2 * _HPG * s_new, _LANES), _F32)],
    )
    o = pl.pallas_call(
        functools.partial(_attn_sample_kernel, npg, past),
        grid_spec=grid_spec,
        out_shape=jax.ShapeDtypeStruct((t, nq), _F32),
        compiler_params=_params("arbitrary", "arbitrary"),
    )(page_table.reshape(-1), *([cache_slc_kv.reshape(n_pool, page, kvw)] * npg), q, gates, o_cmp, selc, seln,
      slc_f, cache_win_kv.reshape(b, n_cache, kvw), win_f, ex)
    return o, cmp_f, slc_f, win_f


def kernel(x_prompt, x_sample, state_ssm, cache_cmp_kv, cache_slc_kv, cache_win_kv, page_table, a_w_in, a_lambda_re, a_lambda_im, a_log_dt, a_b_re, a_b_im, a_c_re, a_c_im, a_d, a_w_glu, kv_w, cmp_pe, cmp_w1, cmp_w2, b_w_qg, b_w_o, moe_w_group, moe_w_expert, moe_w_gate_up, moe_w_down, ln_g, ln_b):
    d = x_prompt.shape[-1]
    bp, lp, _ = x_prompt.shape
    bs, ls, _ = x_sample.shape
    tm_p, tk_p, tc_p = min(512, lp), min(2048, bp * lp), min(64, lp)
    ts = bs * ls

    ar, ai, bbr, bbi = _s5_discretize(a_lambda_re[0], a_lambda_im[0], a_log_dt[0], a_b_re[0], a_b_im[0])
    s5w = _s5_block_weights(ar, ai, bbr, bbi, a_c_re[0], a_c_im[0])
    moe0 = (moe_w_group[0], moe_w_expert[0], moe_w_gate_up[0], moe_w_down[0])
    h0p = jnp.zeros((bp,) + state_ssm.shape[2:], _F32)
    xp, hp = _layer0(x_prompt, h0p, s5w, a_w_in[0], a_d[0], a_w_glu[0], ln_g[0], ln_b[0], moe0, tm_p, tk_p, tc_p)
    xs, hs = _layer0(x_sample, state_ssm[0], s5w, a_w_in[0], a_d[0], a_w_glu[0], ln_g[0], ln_b[0], moe0, ts, ts, ls)

    nsa = _nsa_weights(kv_w, cmp_pe, cmp_w1, cmp_w2, b_w_qg[0], b_w_o[0])
    moe1 = (moe_w_group[1], moe_w_expert[1], moe_w_gate_up[1], moe_w_down[1])
    wr1 = _router_weight(moe1[0], moe1[1])
    kvw = 2 * _N_KV * _HEAD_DIM
    kv_shape = (2, _N_KV, _HEAD_DIM)

    op, cmp_p, slc_p, win_p = _nsa_prompt(xp, bp, lp, nsa, tm_p)
    x1p, route, w1b, w2b, cnt = _mix_ln_route(op, lambda i: (i, 0), (tm_p, d), xp, nsa["w_o"], ln_g[1, 0], ln_b[1, 0],
                                             wr1, tm_p, tk_p, False)
    yp = _moe(x1p, route, w1b, w2b, cnt, moe1[2], moe1[3], ln_g[1, 1], ln_b[1, 1], tk_p)

    os_, cmp_s, slc_s, win_s = _nsa_sample(xs, bs, ls, nsa, cache_cmp_kv, cache_slc_kv, cache_win_kv, page_table)
    x1s, route, w1b, w2b, cnt = _mix_ln_route(os_, lambda i: (i, 0), (ts, d), xs, nsa["w_o"], ln_g[1, 0], ln_b[1, 0],
                                             wr1, ts, ts, False)
    ys = _moe(x1s, route, w1b, w2b, cnt, moe1[2], moe1[3], ln_g[1, 1], ln_b[1, 1], ts)

    n_keep = min(_WINDOW, lp)
    p_win = win_p.reshape((bp, lp) + kv_shape)[:, lp - n_keep:]
    win_all = jnp.concatenate([cache_win_kv, win_s.reshape((bs, ls) + kv_shape).astype(cache_win_kv.dtype)], axis=1)
    s_win = win_all[:, win_all.shape[1] - min(_WINDOW, win_all.shape[1]):]
    return (yp.reshape(bp, lp, d), ys.reshape(bs, ls, d), hp[None],
            cmp_p.reshape((bp, lp) + kv_shape), slc_p.reshape((bp, lp) + kv_shape), p_win,
            hs[None].astype(state_ssm.dtype),
            cmp_s.reshape((bs, ls) + kv_shape), slc_s.reshape((bs, ls) + kv_shape), s_win)
```

```python
import functools
import math

import jax
import jax.numpy as jnp
import numpy as np
from jax import lax
from jax.experimental import pallas as pl
from jax.experimental.pallas import tpu as pltpu

_F32 = jnp.float32
_MXU_DTYPE = jnp.bfloat16
_LANES = 128
_SUBLANES = 8
_VMEM_LIMIT = 56 * 1024 * 1024

_DEPTH = 2
_SSM_W = 16
_SSM_P = 64
_GB = _LANES // _SSM_W
_BSUB = 16
_N_HEADS = 16
_HEAD_DIM = 64
_N_KV = 4
_HPG = _N_HEADS // _N_KV
_N_BRANCH = 3
_CMP_BLOCK = 32
_CMP_STRIDE = 16
_CMP_HIDDEN = 256
_SEL_BLOCK = 64
_N_SELECT = 16
_WINDOW = 512
_ATTN_SCALE = _HEAD_DIM ** -0.5
_FORCE_SCORE = 1e4
_NEG_INF = -1e30
_TINY = 1e-30
_NG = 4
_EPG = 8
_NE = _NG * _EPG
_D_EXPERT = 256
_MOE_ROWS = 128
_ALPHA = (2 * _DEPTH) ** 0.25
_LN_EPS = 1e-5


def _mx(x):
    return x.astype(_MXU_DTYPE)


def _dot(a, b):
    return jnp.dot(_mx(a), _mx(b), preferred_element_type=_F32)


def _dot_nt(a, b):
    return lax.dot_general(_mx(a), _mx(b), (((1,), (1,)), ((), ())), preferred_element_type=_F32)


def _split2(a):
    a1 = _mx(a)
    return a1, _mx(a - a1.astype(_F32))


def _dot_hi(a, b):
    a1, a2 = _split2(a)
    b1, b2 = _split2(b)
    return _dot(a1, b1) + (_dot(a1, b2) + _dot(a2, b1))


def _params(*sem):
    return pltpu.CompilerParams(dimension_semantics=sem, vmem_limit_bytes=_VMEM_LIMIT)


def _layer_norm(v, g, b):
    mu = jnp.mean(v, -1, keepdims=True)
    d = v - mu
    var = jnp.mean(d * d, -1, keepdims=True)
    return d * lax.rsqrt(var + _LN_EPS) * g + b


def _proj_kernel(cols, acts, x_ref, w_ref, *o_refs):
    y = _dot(x_ref[...], w_ref[...])
    for (c0, c1), act, o_ref in zip(cols, acts, o_refs):
        v = y[:, c0:c1]
        if act == "sigmoid":
            v = jax.nn.sigmoid(v)
        o_ref[...] = v.astype(o_ref.dtype)


def _proj(x, w, tm, cols, dtypes, acts=None, grid=None, x_map=None, out_shapes=None, out_maps=None):
    m, k = x.shape
    n = w.shape[1]
    acts = acts or [None] * len(cols)
    if grid is None:
        grid = (m // tm,)
        x_map = lambda i: (i, 0)
        out_shapes = [(m, c1 - c0) for c0, c1 in cols]
        out_maps = [x_map] * len(cols)
    w_map = (lambda *a: (0, 0))
    return pl.pallas_call(
        functools.partial(_proj_kernel, tuple(cols), tuple(acts)),
        grid=grid,
        in_specs=[pl.BlockSpec((tm, k), x_map), pl.BlockSpec((k, n), w_map)],
        out_specs=[pl.BlockSpec((tm, c1 - c0), om) for (c0, c1), om in zip(cols, out_maps)],
        out_shape=[jax.ShapeDtypeStruct(s, d) for s, d in zip(out_shapes, dtypes)],
        compiler_params=_params(*(["arbitrary"] * len(grid))),
        name="proj",
    )(x, _mx(w))


def _s5_disc_kernel(lr_ref, li_ref, ldt_ref, bre_ref, bim_ref, ar_ref, ai_ref, bbr_ref, bbi_ref):
    lr, li = lr_ref[...], li_ref[...]
    dt = jnp.exp(ldt_ref[...])
    mag = jnp.exp(lr * dt)
    a_re, a_im = mag * jnp.cos(li * dt), mag * jnp.sin(li * dt)
    den = lr * lr + li * li
    nr = a_re - 1.0
    z_re = (nr * lr + a_im * li) / den
    z_im = (a_im * lr - nr * li) / den
    b_re, b_im = bre_ref[...], bim_ref[...]
    ar_ref[...] = a_re
    ai_ref[...] = a_im
    bbr_ref[...] = z_re * b_re - z_im * b_im
    bbi_ref[...] = z_re * b_im + z_im * b_re


def _s5_discretize(lam_re, lam_im, log_dt, b_re, b_im):
    g, p = lam_re.shape
    w = b_re.shape[-1]
    n = g * p
    flat = lambda a: a.reshape(1, n)
    b_t = lambda a: a.reshape(n, w).T
    ldt = jnp.repeat(log_dt, p).reshape(1, n)
    outs = pl.pallas_call(
        _s5_disc_kernel,
        out_shape=[jax.ShapeDtypeStruct((1, n), _F32)] * 2 + [jax.ShapeDtypeStruct((w, n), _F32)] * 2,
    )(flat(lam_re), flat(lam_im), ldt, b_t(b_re), b_t(b_im))
    return outs


def _s5_block_weights(abar_re, abar_im, bbt_re, bbt_im, c_re, c_im):
    g = abar_re.shape[1] // _SSM_P
    nb = g // _GB
    eye = jnp.eye(_GB, dtype=_F32)

    def in_mat(bt):
        b4 = bt.reshape(_SSM_W, nb, _GB, _SSM_P)
        return jnp.einsum("wbgp,gh->bgwhp", b4, eye).reshape(nb, _GB * _SSM_W, _GB * _SSM_P)

    def out_mat(c):
        c4 = c.reshape(nb, _GB, _SSM_W, _SSM_P)
        return jnp.einsum("bgwp,gh->bgphw", c4, eye).reshape(nb, _GB * _SSM_P, _GB * _SSM_W)

    bblk = jnp.concatenate([in_mat(bbt_re), in_mat(bbt_im)], axis=2)
    cblk = jnp.concatenate([out_mat(c_re), -out_mat(c_im)], axis=1)
    a = jnp.concatenate([abar_re.reshape(nb, 1, _GB * _SSM_P), abar_im.reshape(nb, 1, _GB * _SSM_P)], axis=2)
    a = jnp.broadcast_to(a, (nb, _BSUB, 2 * _GB * _SSM_P))
    return _mx(bblk), _mx(cblk), a


def _state_to_blocks(h):
    b, g, p, _ = h.shape
    h6 = h.reshape(b // _BSUB, _BSUB, g // _GB, _GB, p, 2)
    return jnp.transpose(h6, (0, 2, 1, 5, 3, 4)).reshape(b // _BSUB, g // _GB, _BSUB, 2 * _GB * p)


def _blocks_to_state(hb):
    nbh, nb, _, _ = hb.shape
    h6 = hb.reshape(nbh, nb, _BSUB, 2, _GB, _SSM_P)
    return jnp.transpose(h6, (0, 2, 1, 4, 5, 3)).reshape(nbh * _BSUB, nb * _GB, _SSM_P, 2)


def _s5_scan_kernel(tc_len, u_ref, bblk_ref, cblk_ref, a_ref, d_ref, h0_ref, g_ref, ht_ref, hbuf, st):
    tci = pl.program_id(2)
    half = _GB * _SSM_P

    @pl.when(tci == 0)
    def _():
        st[...] = h0_ref[...]

    u = u_ref[...]
    hbuf[...] = _dot(u, bblk_ref[...])
    a_re, a_im = a_ref[:, :half], a_ref[:, half:]

    def step(t, carry):
        h_re, h_im = carry
        r0 = pl.multiple_of(t * _BSUB, _BSUB)
        n_re = a_re * h_re - a_im * h_im + hbuf[pl.ds(r0, _BSUB), :half]
        n_im = a_re * h_im + a_im * h_re + hbuf[pl.ds(r0, _BSUB), half:]
        hbuf[pl.ds(r0, _BSUB), :half] = n_re
        hbuf[pl.ds(r0, _BSUB), half:] = n_im
        return n_re, n_im

    h_re, h_im = lax.fori_loop(0, tc_len, step, (st[:, :half], st[:, half:]))
    st[:, :half] = h_re
    st[:, half:] = h_im
    y = _dot(hbuf[...], cblk_ref[...]) + d_ref[...] * u
    g_ref[...] = jax.nn.gelu(y).astype(g_ref.dtype)

    @pl.when(tci == pl.num_programs(2) - 1)
    def _():
        ht_ref[...] = st[...]


def _s5_scan(u_tm, bblk, cblk, a, d_skip, h0_blocks, tc_len):
    nbh, rows, d = u_tm.shape
    nb = d // _LANES
    length = rows // _BSUB
    ntc = length // tc_len
    tr = tc_len * _BSUB
    nstate = 2 * _GB * _SSM_P
    return pl.pallas_call(
        functools.partial(_s5_scan_kernel, tc_len),
        grid=(nbh, nb, ntc),
        in_specs=[
            pl.BlockSpec((None, tr, _LANES), lambda h, b, t: (h, t, b)),
            pl.BlockSpec((None, _LANES, nstate), lambda h, b, t: (b, 0, 0)),
            pl.BlockSpec((None, nstate, _LANES), lambda h, b, t: (b, 0, 0)),
            pl.BlockSpec((None, _BSUB, nstate), lambda h, b, t: (b, 0, 0)),
            pl.BlockSpec((1, _LANES), lambda h, b, t: (0, b)),
            pl.BlockSpec((None, None, _BSUB, nstate), lambda h, b, t: (h, b, 0, 0)),
        ],
        out_specs=[
            pl.BlockSpec((None, tr, _LANES), lambda h, b, t: (h, t, b)),
            pl.BlockSpec((None, None, _BSUB, nstate), lambda h, b, t: (h, b, 0, 0)),
        ],
        out_shape=[jax.ShapeDtypeStruct((nbh, rows, d), _MXU_DTYPE),
                   jax.ShapeDtypeStruct((nbh, nb, _BSUB, nstate), _F32)],
        scratch_shapes=[pltpu.VMEM((tr, nstate), _F32), pltpu.VMEM((_BSUB, nstate), _F32)],
        compiler_params=_params("arbitrary", "arbitrary", "arbitrary"),
        name="s5_scan",
    )(u_tm, bblk, cblk, a, d_skip.reshape(1, d), h0_blocks)


def _route(xn, wr_ref, first_tile, carry_ref, route_ref, w1b_ref, w2b_ref, cnt_ref):
    tm = xn.shape[0]
    logits = _dot_hi(xn, wr_ref[...])
    lane = lax.broadcasted_iota(jnp.int32, logits.shape, 1).astype(_F32)
    neg, big = -jnp.inf, float(_LANES)
    lg = jnp.where(lane < _NG, logits, neg)
    g_top = jnp.max(lg, -1, keepdims=True)
    g_sel = jnp.min(jnp.where(lg == g_top, lane, big), -1, keepdims=True)
    lse = jnp.log(jnp.sum(jnp.exp(lg - g_top), -1, keepdims=True)) + g_top
    g_w = jnp.exp(g_top - lse)
    lo = _NG + g_sel * _EPG
    le = jnp.where((lane >= lo) & (lane < lo + _EPG), logits, neg)
    m1 = jnp.max(le, -1, keepdims=True)
    i1 = jnp.min(jnp.where(le == m1, lane, big), -1, keepdims=True)
    le2 = jnp.where(lane == i1, neg, le)
    m2 = jnp.max(le2, -1, keepdims=True)
    i2 = jnp.min(jnp.where(le2 == m2, lane, big), -1, keepdims=True)
    ex = jnp.exp(m2 - m1)
    w1 = g_w * (1.0 / (1.0 + ex))
    w2 = g_w * (ex / (1.0 + ex))

    @pl.when(first_tile)
    def _():
        carry_ref[...] = jnp.zeros_like(carry_ref)

    hit1, hit2 = lane == i1, lane == i2
    onehot = jnp.where(hit1 | hit2, 1.0, 0.0)
    r = lax.broadcasted_iota(jnp.int32, (tm, tm), 0)
    c = lax.broadcasted_iota(jnp.int32, (tm, tm), 1)
    tri = jnp.where(c < r, 1.0, 0.0)
    pref = _dot(tri, onehot) + carry_ref[...]
    rank1 = jnp.sum(jnp.where(hit1, pref, 0.0), -1, keepdims=True)
    rank2 = jnp.sum(jnp.where(hit2, pref, 0.0), -1, keepdims=True)
    carry = carry_ref[...] + jnp.sum(onehot, 0, keepdims=True)
    carry_ref[...] = carry
    cnt_ref[...] = jnp.broadcast_to(carry, cnt_ref.shape)
    slab = jnp.where(lane == 0, i1 - _NG, jnp.where(lane == 1, i2 - _NG,
                     jnp.where(lane == 2, rank1, jnp.where(lane == 3, rank2, 0.0))))
    route_ref[...] = slab[:, :_SUBLANES]
    w1b_ref[...] = jnp.broadcast_to(w1, w1b_ref.shape)
    w2b_ref[...] = jnp.broadcast_to(w2, w2b_ref.shape)


def _route_specs(tm, tpc, tile_of):
    def tile_map(*ids):
        return (tile_of(*ids), 0)

    def chunk_map(*ids):
        return (tile_of(*ids) // tpc, 0, 0)

    specs = [pl.BlockSpec((tm, _SUBLANES), tile_map), pl.BlockSpec((tm, _LANES), tile_map),
             pl.BlockSpec((tm, _LANES), tile_map), pl.BlockSpec((None, _SUBLANES, _LANES), chunk_map)]
    return specs


def _route_shapes(t, n_chunks):
    return [jax.ShapeDtypeStruct((t, _SUBLANES), _F32), jax.ShapeDtypeStruct((t, _LANES), _F32),
            jax.ShapeDtypeStruct((t, _LANES), _F32), jax.ShapeDtypeStruct((n_chunks, _SUBLANES, _LANES), _F32)]


def _router_weight(w_group, w_expert):
    d = w_group.shape[0]
    pad = jnp.zeros((d, _LANES - _NG - _NE), _F32)
    return jnp.concatenate([w_group, w_expert, pad], axis=1)


def _mix_ln_route_kernel(glu, tpc, f_ref, x_ref, w_ref, lng_ref, lnb_ref, wr_ref,
                         o_ref, route_ref, w1b_ref, w2b_ref, cnt_ref, carry_ref):
    i = pl.program_id(0)
    h = _dot(f_ref[...], w_ref[...])
    if glu:
        d = h.shape[1] // 2
        h = h[:, :d] * jax.nn.sigmoid(h[:, d:])
    xn = _layer_norm(_ALPHA * x_ref[...] + h, lng_ref[...], lnb_ref[...])
    o_ref[...] = xn
    _route(xn, wr_ref, i % tpc == 0, carry_ref, route_ref, w1b_ref, w2b_ref, cnt_ref)


def _mix_ln_route(f, f_map, f_shape_block, x, w, ln_g, ln_b, wr, tm, tk, glu):
    t, d = x.shape
    tpc = tk // tm
    n_chunks = t // tk
    kin, n = w.shape
    row = lambda a: a.reshape(1, d)
    const = lambda i: (0, 0)
    return pl.pallas_call(
        functools.partial(_mix_ln_route_kernel, glu, tpc),
        grid=(t // tm,),
        in_specs=[pl.BlockSpec(f_shape_block, f_map), pl.BlockSpec((tm, d), lambda i: (i, 0)),
                  pl.BlockSpec((kin, n), const), pl.BlockSpec((1, d), const), pl.BlockSpec((1, d), const),
                  pl.BlockSpec((d, _LANES), const)],
        out_specs=[pl.BlockSpec((tm, d), lambda i: (i, 0))] + _route_specs(tm, tpc, lambda i: i),
        out_shape=[jax.ShapeDtypeStruct((t, d), _F32)] + _route_shapes(t, n_chunks),
        scratch_shapes=[pltpu.VMEM((1, _LANES), _F32)],
        compiler_params=_params("arbitrary"),
        name="mix_ln_route",
    )(f, x, _mx(w), row(ln_g), row(ln_b), wr)


def _moe_plan(route, cnt, tk):
    t = route.shape[0]
    n_chunks = t // tk
    ri = route.astype(jnp.int32)
    e1, e2, r1, r2 = (ri[:, k].reshape(n_chunks, tk) for k in range(4))
    counts = cnt[:, 0, _NG:_NG + _NE].astype(jnp.int32)
    padded = (counts + _SUBLANES - 1) // _SUBLANES * _SUBLANES
    offs = jnp.cumsum(padded, axis=1) - padded
    eids = jnp.arange(_NE, dtype=jnp.int32)
    lookup = lambda ee: jnp.sum(jnp.where(ee[..., None] == eids, offs[:, None, :], 0), axis=-1)
    dest = ((lookup(e1) + r1) + ((lookup(e2) + r2) << 16)).reshape(t)
    return dest, offs.reshape(-1), counts.reshape(-1)


def _rows_to_tile(rows):
    n = rows[0].shape[1]
    sub = lax.broadcasted_iota(jnp.int32, (_SUBLANES, n), 0)
    tile = jnp.broadcast_to(rows[0], (_SUBLANES, n))
    for i in range(1, _SUBLANES):
        tile = jnp.where(sub == i, jnp.broadcast_to(rows[i], (_SUBLANES, n)), tile)
    return tile


def _moe_kernel(tk, dest_ref, offs_ref, cnts_ref, x_ref, w1b_ref, w2b_ref, wgu_ref, wd_ref, lng_ref, lnb_ref,
                o_ref, ys_ref, gath_ref, inv_ref):
    c, e = pl.program_id(0), pl.program_id(1)
    n_slots = inv_ref.shape[0]
    d = x_ref.shape[1]

    grp = _SUBLANES

    @pl.when((c == 0) & (e == 0))
    def _():
        gath_ref[...] = jnp.zeros_like(gath_ref)

    @pl.when(e == 0)
    def _():
        def fill(i, _):
            for k in range(grp):
                t = i * grp + k
                dd = dest_ref[c * tk + t]
                inv_ref[dd & 0xFFFF] = t
                inv_ref[dd >> 16] = t
            return 0
        lax.fori_loop(0, tk // grp, fill, 0)

    n = cnts_ref[c * _NE + e]
    off = offs_ref[c * _NE + e]
    pad_end = off + (n + grp - 1) // grp * grp
    for k in range(grp - 1):
        idx = off + n + k
        inv_ref[jnp.where(idx < pad_end, idx, n_slots - 1)] = 0

    def block(j, _):
        base = pl.multiple_of(off + j * _MOE_ROWS, grp)
        n_here = jnp.minimum(n - j * _MOE_ROWS, _MOE_ROWS)

        def gather(g, _):
            r0 = pl.multiple_of(g * grp, grp)
            rows = [x_ref[pl.ds(inv_ref[base + r0 + k], 1), :] for k in range(grp)]
            gath_ref[pl.ds(r0, grp), :] = _rows_to_tile(rows)
            return 0
        lax.fori_loop(0, (n_here + grp - 1) // grp, gather, 0)
        h = _dot(gath_ref[...], wgu_ref[...])
        act = jax.nn.silu(h[:, :_D_EXPERT]) * h[:, _D_EXPERT:]
        ys_ref[pl.ds(base, _MOE_ROWS), :] = _dot(act, wd_ref[...])
        return 0
    lax.fori_loop(0, (n + _MOE_ROWS - 1) // _MOE_ROWS, block, 0)

    @pl.when(e == _NE - 1)
    def _():
        reps = d // _LANES

        def combine(i, _):
            t0 = pl.multiple_of(i * grp, grp)
            y1, y2 = [], []
            for k in range(grp):
                dd = dest_ref[c * tk + t0 + k]
                y1.append(ys_ref[pl.ds(dd & 0xFFFF, 1), :])
                y2.append(ys_ref[pl.ds(dd >> 16, 1), :])
            w1 = jnp.tile(w1b_ref[pl.ds(t0, grp), :], (1, reps))
            w2 = jnp.tile(w2b_ref[pl.ds(t0, grp), :], (1, reps))
            o_ref[pl.ds(t0, grp), :] = (_ALPHA * x_ref[pl.ds(t0, grp), :]
                                        + (_rows_to_tile(y1) * w1 + _rows_to_tile(y2) * w2))
            return 0
        lax.fori_loop(0, tk // grp, combine, 0)
        rt = min(tk, 256)
        for r0 in range(0, tk, rt):
            o_ref[r0:r0 + rt, :] = _layer_norm(o_ref[r0:r0 + rt, :], lng_ref[...], lnb_ref[...])


def _moe(x, route, w1b, w2b, cnt, w_gate_up, w_down, ln_g, ln_b, tk):
    t, d = x.shape
    n_chunks = t // tk
    dest, offs, cnts = _moe_plan(route, cnt, tk)
    n_slots = 2 * tk + _NE * _SUBLANES + _MOE_ROWS
    chunk = lambda c, e, *_: (c, 0)
    const = lambda c, e, *_: (0, 0)
    expert = lambda c, e, *_: (e, 0, 0)
    grid_spec = pltpu.PrefetchScalarGridSpec(
        num_scalar_prefetch=3,
        grid=(n_chunks, _NE),
        in_specs=[pl.BlockSpec((tk, d), chunk, pipeline_mode=pl.Buffered(1)),
                  pl.BlockSpec((tk, _LANES), chunk, pipeline_mode=pl.Buffered(1)),
                  pl.BlockSpec((tk, _LANES), chunk, pipeline_mode=pl.Buffered(1)),
                  pl.BlockSpec((None, d, 2 * _D_EXPERT), expert), pl.BlockSpec((None, _D_EXPERT, d), expert),
                  pl.BlockSpec((1, d), const), pl.BlockSpec((1, d), const)],
        out_specs=pl.BlockSpec((tk, d), chunk),
        scratch_shapes=[pltpu.VMEM((n_slots, d), _F32), pltpu.VMEM((_MOE_ROWS, d), _F32),
                        pltpu.SMEM((n_slots,), jnp.int32)],
    )
    return pl.pallas_call(
        functools.partial(_moe_kernel, tk),
        grid_spec=grid_spec,
        out_shape=jax.ShapeDtypeStruct((t, d), _F32),
        compiler_params=_params("arbitrary", "arbitrary"),
        name="moe",
    )(dest, offs, cnts, x, w1b, w2b, _mx(w_gate_up), _mx(w_down), ln_g.reshape(1, d), ln_b.reshape(1, d))


def _layer0(x, h0, s5w, w_in, d_skip, w_glu, ln_g, ln_b, moe_w, tm, tk, tc_len):
    b, length, d = x.shape
    nbh = b // _BSUB
    bblk, cblk, a = s5w
    x2 = x.reshape(b * length, d)
    if length % tm == 0:
        nt = length // tm
        u_tm = _proj(x2, w_in, tm, [(0, d)], [_F32], grid=(b, nt), x_map=lambda s, j: (s * nt + j, 0),
                     out_shapes=[(nbh * length, _BSUB * d)],
                     out_maps=[lambda s, j: ((s // _BSUB) * nt + j, s % _BSUB)])[0]
        u_tm = u_tm.reshape(nbh, length * _BSUB, d)
    else:
        u = _proj(x2, w_in, b * length, [(0, d)], [_F32])[0]
        u_tm = jnp.transpose(u.reshape(nbh, _BSUB, length, d), (0, 2, 1, 3)).reshape(nbh, length * _BSUB, d)
    g_tm, h_t = _s5_scan(u_tm, bblk, cblk, a, d_skip, _state_to_blocks(h0), tc_len)
    if length % tm == 0:
        nt = length // tm
        f = g_tm.reshape(nbh * length, _BSUB * d)
        f_map = lambda i: (((i // nt) // _BSUB) * nt + i % nt, (i // nt) % _BSUB)
    else:
        f = jnp.transpose(g_tm.reshape(nbh, length, _BSUB, d), (0, 2, 1, 3)).reshape(b * length, d)
        f_map = lambda i: (i, 0)
    w_group, w_expert, w_gate_up, w_down = moe_w
    wr = _router_weight(w_group, w_expert)
    x1, route, w1b, w2b, cnt = _mix_ln_route(f, f_map, (tm, d), x2, w_glu, ln_g[0], ln_b[0], wr, tm, tk, True)
    x2o = _moe(x1, route, w1b, w2b, cnt, w_gate_up, w_down, ln_g[1], ln_b[1], tk)
    return x2o, _blocks_to_state(h_t)


_HALF = _LANES // 2
_N_PAIR = _N_KV // 2
_KV_HALF = _N_KV * _HEAD_DIM


def _head_perm():
    idx = []
    for p in range(_N_PAIR):
        for h in range(_HPG):
            for gs in range(2):
                base = ((2 * p + gs) * _HPG + h) * _HEAD_DIM
                idx.extend(range(base, base + _HEAD_DIM))
    return np.asarray(idx, np.int32)


def _gate_col(p, h, gs, branch):
    return ((2 * p + gs) * _HPG + h) * _N_BRANCH + branch


def _nsa_weights(kv_w, cmp_pe, cmp_w1, cmp_w2, w_qg, w_o):
    nq = _N_HEADS * _HEAD_DIM
    perm = _head_perm()
    w_q = w_qg[:, :nq][:, perm] * _ATTN_SCALE
    w_g = w_qg[:, nq:]
    w_g = jnp.pad(w_g, ((0, 0), (0, _LANES - w_g.shape[1])))
    half = _CMP_STRIDE * _HEAD_DIM
    w1 = jnp.concatenate([cmp_w1[:, :half], cmp_w1[:, half:]], axis=2)
    pe = cmp_pe.reshape(2, 2, 1, half)
    pe = jnp.broadcast_to(pe, (2, 2, _SUBLANES, half))
    return dict(kv_w=kv_w, w_qg=jnp.concatenate([w_q, w_g], axis=1), w_o=w_o[perm], w1=_mx(w1), w2=_mx(cmp_w2), pe=pe)


def _sel_sum_matrix(n_rows, n_cmp, n_sel, lanes):
    m = np.zeros((n_rows, lanes), np.float32)
    for c in range(n_cmp):
        j0 = (c * _CMP_STRIDE) // _SEL_BLOCK
        j1 = (c * _CMP_STRIDE + _CMP_BLOCK - 1) // _SEL_BLOCK
        if j0 < n_sel:
            m[c, j0] += 1.0
        if j1 != j0 and j1 < n_sel:
            m[c, j1] += 1.0
    return jnp.asarray(m, _MXU_DTYPE)


def _expand_matrix(lanes, n_keys):
    j = np.arange(lanes)[:, None]
    key = np.arange(n_keys)[None, :]
    return jnp.asarray((key // _SEL_BLOCK == j).astype(np.float32), _MXU_DTYPE)


_TB_PITCH = 24


def _compress_kernel(npg, transposed, pt_ref, *refs):
    page_refs = refs[:npg + 1]
    w1_ref, w2_ref, pe_ref, o_ref, lhs_ref, tbuf_ref = refs[npg + 1:]
    nlb = 2 * _N_PAIR
    page = page_refs[0].shape[1] if transposed else page_refs[0].shape[0]
    spp = page // _CMP_STRIDE
    seg = npg * spp + _SUBLANES
    rows = _N_KV * seg
    hd = _HEAD_DIM

    @pl.when((pl.program_id(0) == 0) & (pl.program_id(1) == 0))
    def _():
        lhs_ref[...] = jnp.zeros_like(lhs_ref)

    def lane_block(ref, c):
        if transposed:
            return jnp.transpose(ref[c * _LANES:(c + 1) * _LANES, :])
        return ref[:, c * _LANES:(c + 1) * _LANES]

    for p in range(npg + 1):
        nsb = spp if p < npg else 1
        for c in range(nlb):
            blk = lane_block(page_refs[p], c)
            for n in range(nsb):
                tbuf_ref[p, c, n * _TB_PITCH:n * _TB_PITCH + _CMP_STRIDE, :] = blk[n * _CMP_STRIDE:(n + 1) * _CMP_STRIDE]
            for s in range(_CMP_STRIDE):
                xs = tbuf_ref[p, c, pl.ds(s, nsb, stride=_TB_PITCH), :]
                for half in range(2):
                    k, g = divmod(2 * c + half, _N_KV)
                    r0 = g * seg + p * spp
                    lhs_ref[k, r0:r0 + nsb, s * hd:(s + 1) * hd] = xs[:, half * hd:(half + 1) * hd]

    pieces = []
    for k in range(2):
        w_lo, w_hi = w1_ref[k, :, :_CMP_HIDDEN], w1_ref[k, :, _CMP_HIDDEN:]
        lo = _dot(lhs_ref[k, 0:rows, :], w_lo)
        hi = _dot(lhs_ref[k, 1:rows + 1, :], w_hi)
        pe = _dot(pe_ref[k, 0], w_lo)[0:1] + _dot(pe_ref[k, 1], w_hi)[0:1]
        out = _dot(jax.nn.gelu(lo + hi + pe), w2_ref[k])
        pieces += [out[g * seg:g * seg + npg * spp] for g in range(_N_KV)]
    o_ref[...] = jnp.concatenate(pieces, axis=1).astype(o_ref.dtype)


def _compress(pages, table, nsa, npg, transposed):
    b, npp = table.shape
    page, width = (pages.shape[2], pages.shape[1]) if transposed else pages.shape[1:]
    nlb = width // _LANES
    spp = page // _CMP_STRIDE
    seg = npg * spp + _SUBLANES
    half = _CMP_STRIDE * _HEAD_DIM

    def page_map(i):
        return lambda s, j, pt: (pt[s * npp + jnp.minimum(j * npg + i, npp - 1)], 0, 0)

    const3 = lambda s, j, pt: (0, 0, 0)
    const4 = lambda s, j, pt: (0, 0, 0, 0)
    grid_spec = pltpu.PrefetchScalarGridSpec(
        num_scalar_prefetch=1,
        grid=(b, npp // npg),
        in_specs=[pl.BlockSpec((None,) + pages.shape[1:], page_map(i)) for i in range(npg + 1)]
        + [pl.BlockSpec((2, half, 2 * _CMP_HIDDEN), const3), pl.BlockSpec((2, _CMP_HIDDEN, _HEAD_DIM), const3),
           pl.BlockSpec((2, 2, _SUBLANES, half), const4)],
        out_specs=pl.BlockSpec((None, npg * spp, width), lambda s, j, pt: (s, j, 0)),
        scratch_shapes=[pltpu.VMEM((2, _N_KV * seg + _SUBLANES, half), _F32),
                        pltpu.VMEM((npg + 1, nlb, spp * _TB_PITCH, _LANES), _F32)],
    )
    return pl.pallas_call(
        functools.partial(_compress_kernel, npg, transposed),
        grid_spec=grid_spec,
        out_shape=jax.ShapeDtypeStruct((b, npp * spp, width), _MXU_DTYPE),
        compiler_params=_params("arbitrary", "arbitrary"),
        name="nsa_compress",
    )(table.reshape(-1), *([pages] * (npg + 1)), nsa["w1"], nsa["w2"], nsa["pe"])


def _softmax_parts(s, mask):
    sm = jnp.where(mask, s, _NEG_INF)
    m = jnp.max(sm, -1, keepdims=True)
    e = jnp.where(mask, jnp.exp(sm - m), 0.0)
    return e, jnp.maximum(jnp.sum(e, -1, keepdims=True), _TINY)


def _dot_exact_rhs(a, b):
    a1 = _mx(a)
    r1 = a - a1.astype(_F32)
    a2 = _mx(r1)
    a3 = _mx(r1 - a2.astype(_F32))
    return _dot(a1, b) + (_dot(a2, b) + _dot(a3, b))


def _top_k_mask(score, k):
    lane = lax.broadcasted_iota(jnp.int32, score.shape, 1).astype(_F32)
    sel = jnp.zeros(score.shape, _F32)
    for _ in range(k):
        m = jnp.max(score, -1, keepdims=True)
        idx = jnp.min(jnp.where(score == m, lane, float(score.shape[1])), -1, keepdims=True)
        hit = lane == idx
        sel = jnp.where(hit, 1.0, sel)
        score = jnp.where(hit, -jnp.inf, score)
    return sel


def _select_mask(pg, mt_ref, q_pos, n_sel, k_sel):
    ps = _dot_exact_rhs(pg, mt_ref[...])
    j = lax.broadcasted_iota(jnp.int32, ps.shape, 1)
    valid = j * _SEL_BLOCK <= q_pos
    own = jnp.right_shift(q_pos, int(math.log2(_SEL_BLOCK)))
    forced = jnp.where(j == own, 1.0, jnp.where(j == 0, 1.0, 0.0))
    score = jnp.where(valid, ps + _FORCE_SCORE * forced, -1.0)
    score = jnp.where(j < n_sel, score, -jnp.inf)
    return _top_k_mask(score, k_sel)


def _masked_halves(blk):
    low = lax.broadcasted_iota(jnp.int32, blk.shape, 1) < _HALF
    return jnp.concatenate([jnp.where(low, blk, 0.0), jnp.where(low, 0.0, blk)], axis=0)


def _merge_halves(o, rows):
    low = lax.broadcasted_iota(jnp.int32, (rows, _LANES), 1) < _HALF
    return jnp.where(low, o[:rows], o[rows:])


def _gate_block(gates, p, h, branch):
    low = lax.broadcasted_iota(jnp.int32, (gates.shape[0], _LANES), 1) < _HALF
    c0, c1 = _gate_col(p, h, 0, branch), _gate_col(p, h, 1, branch)
    return jnp.where(low, gates[:, c0:c0 + 1], gates[:, c1:c1 + 1])


def _biased_attend(lq, k, v, bias):
    s = _dot_nt(lq, k) + bias
    e = jnp.exp(s - jnp.max(s, -1, keepdims=True))
    den = jnp.maximum(jnp.sum(e, -1, keepdims=True), _TINY)
    return _dot(e, v) * (1.0 / den)


def _rank_select(score, n_sel, k_sel):
    st = jnp.transpose(score)
    top = -(-n_sel // _SUBLANES) * _SUBLANES
    s_top = st[:top]
    ridx = lax.broadcasted_iota(jnp.int32, s_top.shape, 0)
    ahead = jnp.zeros(s_top.shape, _F32)
    for j in range(n_sel):
        sj = st[j:j + 1, :]
        ahead = ahead + jnp.where(sj > s_top, 1.0, 0.0) + jnp.where(sj == s_top, jnp.where(ridx > j, 1.0, 0.0), 0.0)
    sel_t = jnp.where(ahead < k_sel, 1.0, 0.0)
    sel_t = jnp.concatenate([sel_t, jnp.zeros((st.shape[0] - top, st.shape[1]), _F32)], axis=0)
    return jnp.transpose(sel_t)


def _attn_prompt_kernel(tq, tile0, n_sel, k_sel, q_ref, gate_ref, ckv_ref, slc_ref, win_ref, mt_ref, ex_ref, o_ref):
    t0 = (tile0 + pl.program_id(1)) * tq
    nk = slc_ref.shape[0]
    nc = ckv_ref.shape[0]
    q_pos = t0 + lax.broadcasted_iota(jnp.int32, (tq, 1), 0)
    qf = q_ref[...].astype(_F32)
    gates = gate_ref[...]
    c_end = lax.broadcasted_iota(jnp.int32, (1, nc), 1) * _CMP_STRIDE + (_CMP_BLOCK - 1)
    cmask = c_end <= q_pos
    cmask2 = jnp.concatenate([cmask, cmask], axis=0)
    causal_bias = jnp.where(lax.broadcasted_iota(jnp.int32, (1, nk), 1) <= q_pos, 0.0, _NEG_INF)
    wk = min(nk, tq + _WINDOW)
    wstart = pl.multiple_of(jnp.clip(t0 - _WINDOW, 0, nk - wk), tq)
    dist = q_pos - (wstart + lax.broadcasted_iota(jnp.int32, (1, wk), 1))
    wbias = jnp.where(dist >= 0, jnp.where(dist < _WINDOW, 0.0, _NEG_INF), _NEG_INF)
    wbias2 = jnp.concatenate([wbias, wbias], axis=0)
    j = lax.broadcasted_iota(jnp.int32, (tq, _LANES), 1)
    valid = j * _SEL_BLOCK <= q_pos
    own = jnp.right_shift(q_pos, int(math.log2(_SEL_BLOCK)))
    forced = jnp.where(j == own, _FORCE_SCORE, jnp.where(j == 0, _FORCE_SCORE, 0.0))

    for p in range(_N_PAIR):
        ksl = slice(p * _LANES, (p + 1) * _LANES)
        vsl = slice(_KV_HALF + p * _LANES, _KV_HALF + (p + 1) * _LANES)
        kc, vc = ckv_ref[:, ksl], ckv_ref[:, vsl]
        lhs, o_cmp = [], []
        pg = [jnp.zeros((tq, nc), _F32), jnp.zeros((tq, nc), _F32)]
        for h in range(_HPG):
            blk = p * _HPG + h
            lq = _mx(_masked_halves(qf[:, blk * _LANES:(blk + 1) * _LANES]))
            lhs.append(lq)
            e, den = _softmax_parts(_dot_nt(lq, kc), cmask2)
            pc = e * (1.0 / den)
            o_cmp.append(_merge_halves(_dot(pc, vc), tq))
            pg = [pg[0] + pc[:tq], pg[1] + pc[tq:]]
        sbias = []
        for gs in range(2):
            ps = _dot_exact_rhs(pg[gs], mt_ref[...])
            score = jnp.where(valid, ps + forced, -1.0)
            score = jnp.where(j < n_sel, score, -jnp.inf)
            sel = _rank_select(score, n_sel, k_sel)
            picked = _dot(sel, ex_ref[...])
            sbias.append(jnp.minimum(causal_bias, (picked - 1.0) * (-_NEG_INF)))
        sbias2 = jnp.concatenate(sbias, axis=0)
        ks, vs = slc_ref[:, ksl], slc_ref[:, vsl]
        kw, vw = win_ref[pl.ds(wstart, wk), ksl], win_ref[pl.ds(wstart, wk), vsl]
        for h in range(_HPG):
            blk = p * _HPG + h
            o_slc = _merge_halves(_biased_attend(lhs[h], ks, vs, sbias2), tq)
            o_win = _merge_halves(_biased_attend(lhs[h], kw, vw, wbias2), tq)
            o = (_gate_block(gates, p, h, 0) * o_cmp[h] + _gate_block(gates, p, h, 1) * o_slc
                 + _gate_block(gates, p, h, 2) * o_win)
            o_ref[:, blk * _LANES:(blk + 1) * _LANES] = o.astype(o_ref.dtype)


def _nsa_prompt(x, b, length, nsa, tm):
    t, d = x.shape
    kvw = 2 * _KV_HALF
    cmp_f, slc_f, win_f, slc_b, win_b = _proj(
        x, nsa["kv_w"], tm, [(0, kvw), (kvw, 2 * kvw), (2 * kvw, 3 * kvw), (kvw, 2 * kvw), (2 * kvw, 3 * kvw)],
        [_F32, _F32, _F32, _MXU_DTYPE, _MXU_DTYPE])
    nq = _N_HEADS * _HEAD_DIM
    q, gates = _proj(x, nsa["w_qg"], tm, [(0, nq), (nq, nq + _LANES)], [_MXU_DTYPE, _F32], acts=[None, "sigmoid"])
    page = _LANES
    npp = length // page
    table = jnp.arange(b * npp, dtype=jnp.int32).reshape(b, npp)
    ckv = _compress(cmp_f.reshape(b * npp, page, kvw), table, nsa, min(8, npp), False)
    nc = length // _CMP_STRIDE
    tq = _LANES
    nq_t = length // tq
    mt = _sel_sum_matrix(nc, nc - 1, -(-length // _SEL_BLOCK), _LANES)
    slc3, win3 = slc_b.reshape(b, length, kvw), win_b.reshape(b, length, kvw)
    seq = lambda s, i: (s, 0, 0)
    const = lambda s, i: (0, 0)
    n_buckets = 4 if nq_t % 4 == 0 else 1
    tiles = nq_t // n_buckets
    outs = []
    for bk in range(n_buckets):
        tile0 = bk * tiles
        nk = (tile0 + tiles) * tq
        n_sel = nk // _SEL_BLOCK
        nkc = nk // _CMP_STRIDE
        q_map = lambda s, i, tile0=tile0: (s * nq_t + tile0 + i, 0)
        outs.append(pl.pallas_call(
            functools.partial(_attn_prompt_kernel, tq, tile0, n_sel, min(_N_SELECT, n_sel)),
            grid=(b, tiles),
            in_specs=[pl.BlockSpec((tq, nq), q_map), pl.BlockSpec((tq, _LANES), q_map),
                      pl.BlockSpec((None, nkc, kvw), seq), pl.BlockSpec((None, nk, kvw), seq),
                      pl.BlockSpec((None, nk, kvw), seq),
                      pl.BlockSpec((nkc, _LANES), const), pl.BlockSpec((_LANES, nk), const)],
            out_specs=pl.BlockSpec((None, tq, nq), lambda s, i: (s, i, 0)),
            out_shape=jax.ShapeDtypeStruct((b, tiles * tq, nq), _MXU_DTYPE),
            compiler_params=_params("arbitrary", "arbitrary"),
            name=f"nsa_prompt_attn_{bk}",
        )(q, gates, ckv, slc3, win3, mt[:nkc], _expand_matrix(_LANES, nk)))
    o = jnp.concatenate(outs, axis=1).reshape(t, nq)
    return o, cmp_f, slc_f, win_f


def _attn_sample_cmp_kernel(n_sel, k_sel, past, q_ref, ckv_ref, mt_ref, ocmp_ref, sel_ref):
    s_new = q_ref.shape[0]
    nc = ckv_ref.shape[0]
    q_pos = past + lax.broadcasted_iota(jnp.int32, (s_new, 1), 0)
    qf = q_ref[...].astype(_F32)
    c_end = lax.broadcasted_iota(jnp.int32, (1, nc), 1) * _CMP_STRIDE + (_CMP_BLOCK - 1)
    cmask = c_end <= q_pos
    cmask2 = jnp.concatenate([cmask, cmask], axis=0)
    for p in range(_N_PAIR):
        kc = ckv_ref[:, p * _LANES:(p + 1) * _LANES]
        vc = ckv_ref[:, _KV_HALF + p * _LANES:_KV_HALF + (p + 1) * _LANES]
        pg = [jnp.zeros((s_new, nc), _F32), jnp.zeros((s_new, nc), _F32)]
        for h in range(_HPG):
            blk = p * _HPG + h
            lq = _masked_halves(qf[:, blk * _LANES:(blk + 1) * _LANES])
            e, den = _softmax_parts(_dot_nt(lq, kc), cmask2)
            pc = e / den
            ocmp_ref[:, blk * _LANES:(blk + 1) * _LANES] = _merge_halves(_dot(pc, vc), s_new)
            pg = [pg[0] + pc[:s_new], pg[1] + pc[s_new:]]
        for gs in range(2):
            sel_ref[2 * p + gs] = _select_mask(pg[gs], mt_ref, q_pos, n_sel, k_sel)


def _attn_sample_kernel(npg, past, pt_ref, *refs):
    page_refs = refs[:npg]
    (q_ref, gate_ref, ocmp_ref, selc_ref, seln_ref, slcn_ref, winc_ref, winn_ref, ex_ref,
     o_ref, m_ref, l_ref, acc_ref) = refs[npg:]
    j = pl.program_id(1)
    s_new = q_ref.shape[0]
    rows = 2 * _HPG * s_new
    qf = q_ref[...].astype(_F32)

    @pl.when(j == 0)
    def _():
        m_ref[...] = jnp.full(m_ref.shape, _NEG_INF, _F32)
        l_ref[...] = jnp.zeros_like(l_ref)
        acc_ref[...] = jnp.zeros_like(acc_ref)

    def pair_lhs(p):
        halves = [_masked_halves(qf[:, (p * _HPG + h) * _LANES:(p * _HPG + h + 1) * _LANES]) for h in range(_HPG)]
        return jnp.concatenate([hv[:s_new] for hv in halves] + [hv[s_new:] for hv in halves], axis=0)

    def online(p, s, mask, pv):
        sm = jnp.where(mask, s, _NEG_INF)
        m_old = m_ref[p]
        m_new = jnp.maximum(m_old, jnp.max(sm, -1, keepdims=True))
        alpha = jnp.exp(m_old - m_new)
        e = jnp.where(mask, jnp.exp(sm - m_new), 0.0)
        l_ref[p] = alpha * l_ref[p] + jnp.sum(e, -1, keepdims=True)
        acc_ref[p] = alpha * acc_ref[p] + pv(e)
        m_ref[p] = m_new

    for p in range(_N_PAIR):
        ksl = slice(p * _LANES, (p + 1) * _LANES)
        vsl = slice(_KV_HALF + p * _LANES, _KV_HALF + (p + 1) * _LANES)
        lq = pair_lhs(p)
        k_t = jnp.concatenate([r[ksl, :] for r in page_refs], axis=1)
        v_t = jnp.concatenate([r[vsl, :] for r in page_refs], axis=1)
        mask = jnp.concatenate(
            [jnp.tile(_dot(selc_ref[2 * p + gs], ex_ref[...]) > 0.5, (_HPG, 1)) for gs in range(2)], axis=0)
        online(p, _dot(lq, k_t), mask, lambda e, v_t=v_t: _dot_nt(e, v_t))

    @pl.when(j == pl.num_programs(1) - 1)
    def _():
        gates = gate_ref[...]
        tok = lax.broadcasted_iota(jnp.int32, (s_new, 1), 0)
        q_pos = past + jnp.tile(tok, (2 * _HPG, 1))
        newer = lax.broadcasted_iota(jnp.int32, (1, s_new), 1) <= jnp.tile(tok, (2 * _HPG, 1))
        n_cache = winc_ref.shape[1]
        k_pos = past - n_cache + lax.broadcasted_iota(jnp.int32, (1, n_cache + s_new), 1)
        dist = q_pos - k_pos
        wmask = (dist >= 0) & (dist < _WINDOW) & (k_pos >= 0)
        wmask_c, wmask_n = wmask[:, :n_cache], wmask[:, n_cache:]
        for p in range(_N_PAIR):
            ksl = slice(p * _LANES, (p + 1) * _LANES)
            vsl = slice(_KV_HALF + p * _LANES, _KV_HALF + (p + 1) * _LANES)
            lq = pair_lhs(p)
            nmask = jnp.concatenate(
                [jnp.tile(seln_ref[2 * p + gs][:, :s_new] > 0.5, (_HPG, 1)) for gs in range(2)], axis=0) & newer
            online(p, _dot_nt(lq, slcn_ref[:, ksl]), nmask, lambda e, vsl=vsl: _dot(e, slcn_ref[:, vsl]))
            o_slc = acc_ref[p] / jnp.maximum(l_ref[p], _TINY)
            s_c = jnp.where(wmask_c, _dot(lq, winc_ref[ksl, :]), _NEG_INF)
            s_n = jnp.where(wmask_n, _dot_nt(lq, winn_ref[:, ksl]), _NEG_INF)
            m = jnp.maximum(jnp.max(s_c, -1, keepdims=True), jnp.max(s_n, -1, keepdims=True))
            e_c = jnp.where(wmask_c, jnp.exp(s_c - m), 0.0)
            e_n = jnp.where(wmask_n, jnp.exp(s_n - m), 0.0)
            den = jnp.maximum(jnp.sum(e_c, -1, keepdims=True) + jnp.sum(e_n, -1, keepdims=True), _TINY)
            o_win = (_dot_nt(e_c, winc_ref[vsl, :]) + _dot(e_n, winn_ref[:, vsl])) / den
            for h in range(_HPG):
                blk = p * _HPG + h
                lo, hi = h * s_new, (_HPG + h) * s_new
                pick = lambda o: _merge_halves(jnp.concatenate([o[lo:lo + s_new], o[hi:hi + s_new]], axis=0), s_new)
                o = (_gate_block(gates, p, h, 0) * ocmp_ref[:, blk * _LANES:(blk + 1) * _LANES]
                     + _gate_block(gates, p, h, 1) * pick(o_slc) + _gate_block(gates, p, h, 2) * pick(o_win))
                o_ref[:, blk * _LANES:(blk + 1) * _LANES] = o


def _nsa_sample(x, b, s_new, nsa, cache_cmp_kv, cache_slc_kv, cache_win_kv, page_table):
    t, d = x.shape
    kvw = 2 * _KV_HALF
    nq = _N_HEADS * _HEAD_DIM
    n_pool, page = cache_cmp_kv.shape[:2]
    npp = page_table.shape[1]
    past = npp * page
    assert (past + s_new) // _CMP_STRIDE == past // _CMP_STRIDE and past % _SEL_BLOCK == 0
    cmp_f, slc_f, win_f = _proj(x, nsa["kv_w"], t, [(0, kvw), (kvw, 2 * kvw), (2 * kvw, 3 * kvw)], [_F32] * 3)
    q, gates = _proj(x, nsa["w_qg"], t, [(0, nq), (nq, nq + _LANES)], [_F32, _F32], acts=[None, "sigmoid"])
    kv_t = lambda c: jnp.transpose(c, (0, 2, 3, 4, 1)).reshape(c.shape[0], kvw, c.shape[1])
    ckv = _compress(kv_t(cache_cmp_kv), page_table, nsa, min(8, npp), True)
    nc = past // _CMP_STRIDE
    n_sel = -(-(past + s_new) // _SEL_BLOCK)
    sel_lanes = -(-n_sel // _LANES) * _LANES
    mt = _sel_sum_matrix(nc, nc - 1, n_sel, sel_lanes)
    seq2 = lambda s: (s, 0)
    o_cmp, sel = pl.pallas_call(
        functools.partial(_attn_sample_cmp_kernel, n_sel, min(_N_SELECT, n_sel), past),
        grid=(b,),
        in_specs=[pl.BlockSpec((s_new, nq), seq2), pl.BlockSpec((None, nc, kvw), lambda s: (s, 0, 0)),
                  pl.BlockSpec((nc, sel_lanes), lambda s: (0, 0))],
        out_specs=[pl.BlockSpec((s_new, nq), seq2),
                   pl.BlockSpec((None, _N_KV, s_new, sel_lanes), lambda s: (s, 0, 0, 0))],
        out_shape=[jax.ShapeDtypeStruct((t, nq), _F32), jax.ShapeDtypeStruct((b, _N_KV, s_new, sel_lanes), _F32)],
        compiler_params=_params("arbitrary"),
    )(q, ckv, mt)

    npg = min(16, npp)
    n_chunks = npp // npg
    bpc = npg * page // _SEL_BLOCK
    n_past_blocks = past // _SEL_BLOCK
    selc = sel[..., :n_past_blocks].reshape(b, _N_KV, s_new, n_chunks, bpc)
    selc = jnp.pad(jnp.transpose(selc, (0, 3, 1, 2, 4)), ((0, 0),) * 4 + ((0, _LANES - bpc),))
    seln = jnp.broadcast_to(sel[..., n_past_blocks:n_past_blocks + 1], (b, _N_KV, s_new, _LANES))
    ex = _expand_matrix(_LANES, npg * page)
    n_cache = cache_win_kv.shape[1]

    def page_map(i):
        return lambda s, j, pt: (pt[s * npp + j * npg + i], 0, 0)

    seq = lambda s, j, pt: (s, 0)
    seq3 = lambda s, j, pt: (s, 0, 0)
    grid_spec = pltpu.PrefetchScalarGridSpec(
        num_scalar_prefetch=1,
        grid=(b, n_chunks),
        in_specs=[pl.BlockSpec((None, kvw, page), page_map(i)) for i in range(npg)]
        + [pl.BlockSpec((s_new, nq), seq), pl.BlockSpec((s_new, _LANES), seq), pl.BlockSpec((s_new, nq), seq),
           pl.BlockSpec((None, None, _N_KV, s_new, _LANES), lambda s, j, pt: (s, j, 0, 0, 0)),
           pl.BlockSpec((None, _N_KV, s_new, _LANES), lambda s, j, pt: (s, 0, 0, 0)),
           pl.BlockSpec((s_new, kvw), seq), pl.BlockSpec((None, kvw, n_cache), seq3), pl.BlockSpec((s_new, kvw), seq),
           pl.BlockSpec((_LANES, npg * page), lambda s, j, pt: (0, 0))],
        out_specs=pl.BlockSpec((s_new, nq), seq),
        scratch_shapes=[pltpu.VMEM((_N_PAIR, 2 * _HPG * s_new, 1), _F32), pltpu.VMEM((_N_PAIR, 2 * _HPG * s_new, 1), _F32),
                        pltpu.VMEM((_N_PAIR, 2 * _HPG * s_new, _LANES), _F32)],
    )
    o = pl.pallas_call(
        functools.partial(_attn_sample_kernel, npg, past),
        grid_spec=grid_spec,
        out_shape=jax.ShapeDtypeStruct((t, nq), _F32),
        compiler_params=_params("arbitrary", "arbitrary"),
        name="nsa_sample_attn",
    )(page_table.reshape(-1), *([kv_t(cache_slc_kv)] * npg), q, gates, o_cmp, selc, seln,
      slc_f, kv_t(cache_win_kv), win_f, ex)
    return o, cmp_f, slc_f, win_f


def kernel(x_prompt, x_sample, state_ssm, cache_cmp_kv, cache_slc_kv, cache_win_kv, page_table, a_w_in, a_lambda_re, a_lambda_im, a_log_dt, a_b_re, a_b_im, a_c_re, a_c_im, a_d, a_w_glu, kv_w, cmp_pe, cmp_w1, cmp_w2, b_w_qg, b_w_o, moe_w_group, moe_w_expert, moe_w_gate_up, moe_w_down, ln_g, ln_b):
    d = x_prompt.shape[-1]
    bp, lp, _ = x_prompt.shape
    bs, ls, _ = x_sample.shape
    tm_p, tk_p, tc_p = min(512, lp), min(2048, bp * lp), min(64, lp)
    ts = bs * ls

    ar, ai, bbr, bbi = _s5_discretize(a_lambda_re[0], a_lambda_im[0], a_log_dt[0], a_b_re[0], a_b_im[0])
    s5w = _s5_block_weights(ar, ai, bbr, bbi, a_c_re[0], a_c_im[0])
    moe0 = (moe_w_group[0], moe_w_expert[0], moe_w_gate_up[0], moe_w_down[0])
    h0p = jnp.zeros((bp,) + state_ssm.shape[2:], _F32)
    xp, hp = _layer0(x_prompt, h0p, s5w, a_w_in[0], a_d[0], a_w_glu[0], ln_g[0], ln_b[0], moe0, tm_p, tk_p, tc_p)
    xs, hs = _layer0(x_sample, state_ssm[0], s5w, a_w_in[0], a_d[0], a_w_glu[0], ln_g[0], ln_b[0], moe0, ts, ts, ls)

    nsa = _nsa_weights(kv_w, cmp_pe, cmp_w1, cmp_w2, b_w_qg[0], b_w_o[0])
    moe1 = (moe_w_group[1], moe_w_expert[1], moe_w_gate_up[1], moe_w_down[1])
    wr1 = _router_weight(moe1[0], moe1[1])
    kvw = 2 * _N_KV * _HEAD_DIM
    kv_shape = (2, _N_KV, _HEAD_DIM)

    op, cmp_p, slc_p, win_p = _nsa_prompt(xp, bp, lp, nsa, tm_p)
    x1p, route, w1b, w2b, cnt = _mix_ln_route(op, lambda i: (i, 0), (tm_p, d), xp, nsa["w_o"], ln_g[1, 0], ln_b[1, 0],
                                             wr1, tm_p, tk_p, False)
    yp = _moe(x1p, route, w1b, w2b, cnt, moe1[2], moe1[3], ln_g[1, 1], ln_b[1, 1], tk_p)

    os_, cmp_s, slc_s, win_s = _nsa_sample(xs, bs, ls, nsa, cache_cmp_kv, cache_slc_kv, cache_win_kv, page_table)
    x1s, route, w1b, w2b, cnt = _mix_ln_route(os_, lambda i: (i, 0), (ts, d), xs, nsa["w_o"], ln_g[1, 0], ln_b[1, 0],
                                             wr1, ts, ts, False)
    ys = _moe(x1s, route, w1b, w2b, cnt, moe1[2], moe1[3], ln_g[1, 1], ln_b[1, 1], ts)

    n_keep = min(_WINDOW, lp)
    p_win = win_p.reshape((bp, lp) + kv_shape)[:, lp - n_keep:]
    win_all = jnp.concatenate([cache_win_kv, win_s.reshape((bs, ls) + kv_shape).astype(cache_win_kv.dtype)], axis=1)
    s_win = win_all[:, win_all.shape[1] - min(_WINDOW, win_all.shape[1]):]
    return (yp.reshape(bp, lp, d), ys.reshape(bs, ls, d), hp[None],
            cmp_p.reshape((bp, lp) + kv_shape), slc_p.reshape((bp, lp) + kv_shape), p_win,
            hs[None].astype(state_ssm.dtype),
            cmp_s.reshape((bs, ls) + kv_shape), slc_s.reshape((bs, ls) + kv_shape), s_win)
```

```python
import functools
import math

import jax
import jax.numpy as jnp
import numpy as np
from jax import lax
from jax.experimental import pallas as pl
from jax.experimental.pallas import tpu as pltpu

_F32 = jnp.float32
_MXU_DTYPE = jnp.bfloat16
_LANES = 128
_SUBLANES = 8
_VMEM_LIMIT = 56 * 1024 * 1024

_DEPTH = 2
_SSM_W = 16
_SSM_P = 64
_GB = _LANES // _SSM_W
_BSUB = 16
_N_HEADS = 16
_HEAD_DIM = 64
_N_KV = 4
_HPG = _N_HEADS // _N_KV
_N_BRANCH = 3
_CMP_BLOCK = 32
_CMP_STRIDE = 16
_CMP_HIDDEN = 256
_SEL_BLOCK = 64
_N_SELECT = 16
_WINDOW = 512
_ATTN_SCALE = _HEAD_DIM ** -0.5
_FORCE_SCORE = 1e4
_NEG_INF = -1e30
_TINY = 1e-30
_NG = 4
_EPG = 8
_NE = _NG * _EPG
_D_EXPERT = 256
_MOE_ROWS = 128
_ALPHA = (2 * _DEPTH) ** 0.25
_LN_EPS = 1e-5


def _mx(x):
    return x.astype(_MXU_DTYPE)


def _dot(a, b):
    return jnp.dot(_mx(a), _mx(b), preferred_element_type=_F32)


def _dot_nt(a, b):
    return lax.dot_general(_mx(a), _mx(b), (((1,), (1,)), ((), ())), preferred_element_type=_F32)


def _split2(a):
    a1 = _mx(a)
    return a1, _mx(a - a1.astype(_F32))


def _dot_hi(a, b):
    a1, a2 = _split2(a)
    b1, b2 = _split2(b)
    return _dot(a1, b1) + (_dot(a1, b2) + _dot(a2, b1))


def _params(*sem):
    return pltpu.CompilerParams(dimension_semantics=sem, vmem_limit_bytes=_VMEM_LIMIT)


def _layer_norm(v, g, b):
    mu = jnp.mean(v, -1, keepdims=True)
    d = v - mu
    var = jnp.mean(d * d, -1, keepdims=True)
    return d * lax.rsqrt(var + _LN_EPS) * g + b


def _proj_kernel(cols, acts, x_ref, w_ref, *o_refs):
    y = _dot(x_ref[...], w_ref[...])
    for (c0, c1), act, o_ref in zip(cols, acts, o_refs):
        v = y[:, c0:c1]
        if act == "sigmoid":
            v = jax.nn.sigmoid(v)
        o_ref[...] = v.astype(o_ref.dtype)


def _proj(x, w, tm, cols, dtypes, acts=None, grid=None, x_map=None, out_shapes=None, out_maps=None):
    m, k = x.shape
    n = w.shape[1]
    acts = acts or [None] * len(cols)
    if grid is None:
        grid = (m // tm,)
        x_map = lambda i: (i, 0)
        out_shapes = [(m, c1 - c0) for c0, c1 in cols]
        out_maps = [x_map] * len(cols)
    w_map = (lambda *a: (0, 0))
    return pl.pallas_call(
        functools.partial(_proj_kernel, tuple(cols), tuple(acts)),
        grid=grid,
        in_specs=[pl.BlockSpec((tm, k), x_map), pl.BlockSpec((k, n), w_map)],
        out_specs=[pl.BlockSpec((tm, c1 - c0), om) for (c0, c1), om in zip(cols, out_maps)],
        out_shape=[jax.ShapeDtypeStruct(s, d) for s, d in zip(out_shapes, dtypes)],
        compiler_params=_params(*(["arbitrary"] * len(grid))),
        name="proj",
    )(x, _mx(w))


def _s5_disc_kernel(lr_ref, li_ref, ldt_ref, bre_ref, bim_ref, ar_ref, ai_ref, bbr_ref, bbi_ref):
    lr, li = lr_ref[...], li_ref[...]
    dt = jnp.exp(ldt_ref[...])
    mag = jnp.exp(lr * dt)
    a_re, a_im = mag * jnp.cos(li * dt), mag * jnp.sin(li * dt)
    den = lr * lr + li * li
    nr = a_re - 1.0
    z_re = (nr * lr + a_im * li) / den
    z_im = (a_im * lr - nr * li) / den
    b_re, b_im = bre_ref[...], bim_ref[...]
    ar_ref[...] = a_re
    ai_ref[...] = a_im
    bbr_ref[...] = z_re * b_re - z_im * b_im
    bbi_ref[...] = z_re * b_im + z_im * b_re


def _s5_discretize(lam_re, lam_im, log_dt, b_re, b_im):
    g, p = lam_re.shape
    w = b_re.shape[-1]
    n = g * p
    flat = lambda a: a.reshape(1, n)
    b_t = lambda a: a.reshape(n, w).T
    ldt = jnp.repeat(log_dt, p).reshape(1, n)
    outs = pl.pallas_call(
        _s5_disc_kernel,
        out_shape=[jax.ShapeDtypeStruct((1, n), _F32)] * 2 + [jax.ShapeDtypeStruct((w, n), _F32)] * 2,
    )(flat(lam_re), flat(lam_im), ldt, b_t(b_re), b_t(b_im))
    return outs


def _s5_block_weights(abar_re, abar_im, bbt_re, bbt_im, c_re, c_im):
    g = abar_re.shape[1] // _SSM_P
    nb = g // _GB
    eye = jnp.eye(_GB, dtype=_F32)

    def in_mat(bt):
        b4 = bt.reshape(_SSM_W, nb, _GB, _SSM_P)
        return jnp.einsum("wbgp,gh->bgwhp", b4, eye).reshape(nb, _GB * _SSM_W, _GB * _SSM_P)

    def out_mat(c):
        c4 = c.reshape(nb, _GB, _SSM_W, _SSM_P)
        return jnp.einsum("bgwp,gh->bgphw", c4, eye).reshape(nb, _GB * _SSM_P, _GB * _SSM_W)

    bblk = jnp.concatenate([in_mat(bbt_re), in_mat(bbt_im)], axis=2)
    cblk = jnp.concatenate([out_mat(c_re), -out_mat(c_im)], axis=1)
    a = jnp.concatenate([abar_re.reshape(nb, 1, _GB * _SSM_P), abar_im.reshape(nb, 1, _GB * _SSM_P)], axis=2)
    a = jnp.broadcast_to(a, (nb, _BSUB, 2 * _GB * _SSM_P))
    return _mx(bblk), _mx(cblk), a


def _state_to_blocks(h):
    b, g, p, _ = h.shape
    h6 = h.reshape(b // _BSUB, _BSUB, g // _GB, _GB, p, 2)
    return jnp.transpose(h6, (0, 2, 1, 5, 3, 4)).reshape(b // _BSUB, g // _GB, _BSUB, 2 * _GB * p)


def _blocks_to_state(hb):
    nbh, nb, _, _ = hb.shape
    h6 = hb.reshape(nbh, nb, _BSUB, 2, _GB, _SSM_P)
    return jnp.transpose(h6, (0, 2, 1, 4, 5, 3)).reshape(nbh * _BSUB, nb * _GB, _SSM_P, 2)


def _s5_scan_kernel(tc_len, u_ref, bblk_ref, cblk_ref, a_ref, d_ref, h0_ref, g_ref, ht_ref, hbuf, st):
    tci = pl.program_id(2)
    half = _GB * _SSM_P

    @pl.when(tci == 0)
    def _():
        st[...] = h0_ref[...]

    u = u_ref[...]
    hbuf[...] = _dot(u, bblk_ref[...])
    a_re, a_im = a_ref[:, :half], a_ref[:, half:]

    def step(t, carry):
        h_re, h_im = carry
        r0 = pl.multiple_of(t * _BSUB, _BSUB)
        n_re = a_re * h_re - a_im * h_im + hbuf[pl.ds(r0, _BSUB), :half]
        n_im = a_re * h_im + a_im * h_re + hbuf[pl.ds(r0, _BSUB), half:]
        hbuf[pl.ds(r0, _BSUB), :half] = n_re
        hbuf[pl.ds(r0, _BSUB), half:] = n_im
        return n_re, n_im

    h_re, h_im = lax.fori_loop(0, tc_len, step, (st[:, :half], st[:, half:]))
    st[:, :half] = h_re
    st[:, half:] = h_im
    y = _dot(hbuf[...], cblk_ref[...]) + d_ref[...] * u
    g_ref[...] = jax.nn.gelu(y).astype(g_ref.dtype)

    @pl.when(tci == pl.num_programs(2) - 1)
    def _():
        ht_ref[...] = st[...]


def _s5_scan(u_tm, bblk, cblk, a, d_skip, h0_blocks, tc_len):
    nbh, rows, d = u_tm.shape
    nb = d // _LANES
    length = rows // _BSUB
    ntc = length // tc_len
    tr = tc_len * _BSUB
    nstate = 2 * _GB * _SSM_P
    return pl.pallas_call(
        functools.partial(_s5_scan_kernel, tc_len),
        grid=(nbh, nb, ntc),
        in_specs=[
            pl.BlockSpec((None, tr, _LANES), lambda h, b, t: (h, t, b)),
            pl.BlockSpec((None, _LANES, nstate), lambda h, b, t: (b, 0, 0)),
            pl.BlockSpec((None, nstate, _LANES), lambda h, b, t: (b, 0, 0)),
            pl.BlockSpec((None, _BSUB, nstate), lambda h, b, t: (b, 0, 0)),
            pl.BlockSpec((1, _LANES), lambda h, b, t: (0, b)),
            pl.BlockSpec((None, None, _BSUB, nstate), lambda h, b, t: (h, b, 0, 0)),
        ],
        out_specs=[
            pl.BlockSpec((None, tr, _LANES), lambda h, b, t: (h, t, b)),
            pl.BlockSpec((None, None, _BSUB, nstate), lambda h, b, t: (h, b, 0, 0)),
        ],
        out_shape=[jax.ShapeDtypeStruct((nbh, rows, d), _MXU_DTYPE),
                   jax.ShapeDtypeStruct((nbh, nb, _BSUB, nstate), _F32)],
        scratch_shapes=[pltpu.VMEM((tr, nstate), _F32), pltpu.VMEM((_BSUB, nstate), _F32)],
        compiler_params=_params("arbitrary", "arbitrary", "arbitrary"),
        name="s5_scan",
    )(u_tm, bblk, cblk, a, d_skip.reshape(1, d), h0_blocks)


def _route(xn, wr_ref, first_tile, carry_ref, route_ref, w1b_ref, w2b_ref, cnt_ref):
    tm = xn.shape[0]
    logits = _dot_hi(xn, wr_ref[...])
    lane = lax.broadcasted_iota(jnp.int32, logits.shape, 1).astype(_F32)
    neg, big = -jnp.inf, float(_LANES)
    lg = jnp.where(lane < _NG, logits, neg)
    g_top = jnp.max(lg, -1, keepdims=True)
    g_sel = jnp.min(jnp.where(lg == g_top, lane, big), -1, keepdims=True)
    lse = jnp.log(jnp.sum(jnp.exp(lg - g_top), -1, keepdims=True)) + g_top
    g_w = jnp.exp(g_top - lse)
    lo = _NG + g_sel * _EPG
    le = jnp.where((lane >= lo) & (lane < lo + _EPG), logits, neg)
    m1 = jnp.max(le, -1, keepdims=True)
    i1 = jnp.min(jnp.where(le == m1, lane, big), -1, keepdims=True)
    le2 = jnp.where(lane == i1, neg, le)
    m2 = jnp.max(le2, -1, keepdims=True)
    i2 = jnp.min(jnp.where(le2 == m2, lane, big), -1, keepdims=True)
    ex = jnp.exp(m2 - m1)
    w1 = g_w * (1.0 / (1.0 + ex))
    w2 = g_w * (ex / (1.0 + ex))

    @pl.when(first_tile)
    def _():
        carry_ref[...] = jnp.zeros_like(carry_ref)

    hit1, hit2 = lane == i1, lane == i2
    onehot = jnp.where(hit1 | hit2, 1.0, 0.0)
    r = lax.broadcasted_iota(jnp.int32, (tm, tm), 0)
    c = lax.broadcasted_iota(jnp.int32, (tm, tm), 1)
    tri = jnp.where(c < r, 1.0, 0.0)
    pref = _dot(tri, onehot) + carry_ref[...]
    rank1 = jnp.sum(jnp.where(hit1, pref, 0.0), -1, keepdims=True)
    rank2 = jnp.sum(jnp.where(hit2, pref, 0.0), -1, keepdims=True)
    carry = carry_ref[...] + jnp.sum(onehot, 0, keepdims=True)
    carry_ref[...] = carry
    cnt_ref[...] = jnp.broadcast_to(carry, cnt_ref.shape)
    slab = jnp.where(lane == 0, i1 - _NG, jnp.where(lane == 1, i2 - _NG,
                     jnp.where(lane == 2, rank1, jnp.where(lane == 3, rank2, 0.0))))
    route_ref[...] = slab[:, :_SUBLANES]
    w1b_ref[...] = jnp.broadcast_to(w1, w1b_ref.shape)
    w2b_ref[...] = jnp.broadcast_to(w2, w2b_ref.shape)


def _route_specs(tm, tpc, tile_of):
    def tile_map(*ids):
        return (tile_of(*ids), 0)

    def chunk_map(*ids):
        return (tile_of(*ids) // tpc, 0, 0)

    specs = [pl.BlockSpec((tm, _SUBLANES), tile_map), pl.BlockSpec((tm, _LANES), tile_map),
             pl.BlockSpec((tm, _LANES), tile_map), pl.BlockSpec((None, _SUBLANES, _LANES), chunk_map)]
    return specs


def _route_shapes(t, n_chunks):
    return [jax.ShapeDtypeStruct((t, _SUBLANES), _F32), jax.ShapeDtypeStruct((t, _LANES), _F32),
            jax.ShapeDtypeStruct((t, _LANES), _F32), jax.ShapeDtypeStruct((n_chunks, _SUBLANES, _LANES), _F32)]


def _router_weight(w_group, w_expert):
    d = w_group.shape[0]
    pad = jnp.zeros((d, _LANES - _NG - _NE), _F32)
    return jnp.concatenate([w_group, w_expert, pad], axis=1)


def _mix_ln_route_kernel(glu, tpc, f_ref, x_ref, w_ref, lng_ref, lnb_ref, wr_ref,
                         o_ref, route_ref, w1b_ref, w2b_ref, cnt_ref, carry_ref):
    i = pl.program_id(0)
    h = _dot(f_ref[...], w_ref[...])
    if glu:
        d = h.shape[1] // 2
        h = h[:, :d] * jax.nn.sigmoid(h[:, d:])
    xn = _layer_norm(_ALPHA * x_ref[...] + h, lng_ref[...], lnb_ref[...])
    o_ref[...] = xn
    _route(xn, wr_ref, i % tpc == 0, carry_ref, route_ref, w1b_ref, w2b_ref, cnt_ref)


def _mix_ln_route(f, f_map, f_shape_block, x, w, ln_g, ln_b, wr, tm, tk, glu):
    t, d = x.shape
    tpc = tk // tm
    n_chunks = t // tk
    kin, n = w.shape
    row = lambda a: a.reshape(1, d)
    const = lambda i: (0, 0)
    return pl.pallas_call(
        functools.partial(_mix_ln_route_kernel, glu, tpc),
        grid=(t // tm,),
        in_specs=[pl.BlockSpec(f_shape_block, f_map), pl.BlockSpec((tm, d), lambda i: (i, 0)),
                  pl.BlockSpec((kin, n), const), pl.BlockSpec((1, d), const), pl.BlockSpec((1, d), const),
                  pl.BlockSpec((d, _LANES), const)],
        out_specs=[pl.BlockSpec((tm, d), lambda i: (i, 0))] + _route_specs(tm, tpc, lambda i: i),
        out_shape=[jax.ShapeDtypeStruct((t, d), _F32)] + _route_shapes(t, n_chunks),
        scratch_shapes=[pltpu.VMEM((1, _LANES), _F32)],
        compiler_params=_params("arbitrary"),
        name="mix_ln_route",
    )(f, x, _mx(w), row(ln_g), row(ln_b), wr)


def _moe_plan(route, cnt, tk):
    t = route.shape[0]
    n_chunks = t // tk
    ri = route.astype(jnp.int32)
    e1, e2, r1, r2 = (ri[:, k].reshape(n_chunks, tk) for k in range(4))
    counts = cnt[:, 0, _NG:_NG + _NE].astype(jnp.int32)
    padded = (counts + _SUBLANES - 1) // _SUBLANES * _SUBLANES
    offs = jnp.cumsum(padded, axis=1) - padded
    eids = jnp.arange(_NE, dtype=jnp.int32)
    lookup = lambda ee: jnp.sum(jnp.where(ee[..., None] == eids, offs[:, None, :], 0), axis=-1)
    dest = ((lookup(e1) + r1) + ((lookup(e2) + r2) << 16)).reshape(t)
    return dest, offs.reshape(-1), counts.reshape(-1)


def _moe_kernel(tk, dest_ref, offs_ref, cnts_ref, x_ref, w1b_ref, w2b_ref, wgu_ref, wd_ref, lng_ref, lnb_ref,
                o_ref, ys_ref, gath_ref, inv_ref):
    c, e = pl.program_id(0), pl.program_id(1)
    n_slots = inv_ref.shape[0]
    d = x_ref.shape[1]

    grp = _SUBLANES

    @pl.when((c == 0) & (e == 0))
    def _():
        gath_ref[...] = jnp.zeros_like(gath_ref)

    @pl.when(e == 0)
    def _():
        def fill(i, _):
            for k in range(grp):
                t = i * grp + k
                dd = dest_ref[c * tk + t]
                inv_ref[dd & 0xFFFF] = t
                inv_ref[dd >> 16] = t
            return 0
        lax.fori_loop(0, tk // grp, fill, 0)

    n = cnts_ref[c * _NE + e]
    off = offs_ref[c * _NE + e]
    pad_end = off + (n + grp - 1) // grp * grp
    for k in range(grp - 1):
        idx = off + n + k
        inv_ref[jnp.where(idx < pad_end, idx, n_slots - 1)] = 0

    def block(j, _):
        base = pl.multiple_of(off + j * _MOE_ROWS, grp)
        n_here = jnp.minimum(n - j * _MOE_ROWS, _MOE_ROWS)

        def gather(g, _):
            for k in range(grp):
                gath_ref[g, k:k + 1, :] = x_ref[pl.ds(inv_ref[base + g * grp + k], 1), :]
            return 0
        lax.fori_loop(0, (n_here + grp - 1) // grp, gather, 0)
        h = _dot(gath_ref[...].reshape(_MOE_ROWS, d), wgu_ref[...])
        act = jax.nn.silu(h[:, :_D_EXPERT]) * h[:, _D_EXPERT:]
        ys_ref[pl.ds(base, _MOE_ROWS), :] = _dot(act, wd_ref[...])
        return 0
    lax.fori_loop(0, (n + _MOE_ROWS - 1) // _MOE_ROWS, block, 0)

    @pl.when(e == _NE - 1)
    def _():
        reps = d // _LANES

        def combine(i, _):
            for k in range(grp):
                t = i * grp + k
                dd = dest_ref[c * tk + t]
                y1 = ys_ref[pl.ds(dd & 0xFFFF, 1), :]
                y2 = ys_ref[pl.ds(dd >> 16, 1), :]
                w1 = jnp.tile(w1b_ref[i, k:k + 1, :], (1, reps))
                w2 = jnp.tile(w2b_ref[i, k:k + 1, :], (1, reps))
                o_ref[i, k:k + 1, :] = _ALPHA * x_ref[pl.ds(t, 1), :] + (y1 * w1 + y2 * w2)
            return 0
        lax.fori_loop(0, tk // grp, combine, 0)
        gt = min(tk, 256) // grp
        for g0 in range(0, tk // grp, gt):
            v = o_ref[g0:g0 + gt].reshape(gt * grp, d)
            o_ref[g0:g0 + gt] = _layer_norm(v, lng_ref[...], lnb_ref[...]).reshape(gt, grp, d)


def _moe(x, route, w1b, w2b, cnt, w_gate_up, w_down, ln_g, ln_b, tk):
    t, d = x.shape
    n_chunks = t // tk
    dest, offs, cnts = _moe_plan(route, cnt, tk)
    n_slots = 2 * tk + _NE * _SUBLANES + _MOE_ROWS
    grp = _SUBLANES
    chunk = lambda c, e, *_: (c, 0)
    chunk3 = lambda c, e, *_: (c, 0, 0)
    const = lambda c, e, *_: (0, 0)
    expert = lambda c, e, *_: (e, 0, 0)
    grid_spec = pltpu.PrefetchScalarGridSpec(
        num_scalar_prefetch=3,
        grid=(n_chunks, _NE),
        in_specs=[pl.BlockSpec((tk, d), chunk, pipeline_mode=pl.Buffered(1)),
                  pl.BlockSpec((tk // grp, grp, _LANES), chunk3, pipeline_mode=pl.Buffered(1)),
                  pl.BlockSpec((tk // grp, grp, _LANES), chunk3, pipeline_mode=pl.Buffered(1)),
                  pl.BlockSpec((None, d, 2 * _D_EXPERT), expert), pl.BlockSpec((None, _D_EXPERT, d), expert),
                  pl.BlockSpec((1, d), const), pl.BlockSpec((1, d), const)],
        out_specs=pl.BlockSpec((tk // grp, grp, d), chunk3),
        scratch_shapes=[pltpu.VMEM((n_slots, d), _F32), pltpu.VMEM((_MOE_ROWS // grp, grp, d), _F32),
                        pltpu.SMEM((n_slots,), jnp.int32)],
    )
    return pl.pallas_call(
        functools.partial(_moe_kernel, tk),
        grid_spec=grid_spec,
        out_shape=jax.ShapeDtypeStruct((t // grp, grp, d), _F32),
        compiler_params=_params("arbitrary", "arbitrary"),
        name="moe",
    )(dest, offs, cnts, x, w1b.reshape(t // grp, grp, _LANES), w2b.reshape(t // grp, grp, _LANES),
      _mx(w_gate_up), _mx(w_down), ln_g.reshape(1, d), ln_b.reshape(1, d)).reshape(t, d)


def _layer0(x, h0, s5w, w_in, d_skip, w_glu, ln_g, ln_b, moe_w, tm, tk, tc_len):
    b, length, d = x.shape
    nbh = b // _BSUB
    bblk, cblk, a = s5w
    x2 = x.reshape(b * length, d)
    if length % tm == 0:
        nt = length // tm
        u_tm = _proj(x2, w_in, tm, [(0, d)], [_F32], grid=(b, nt), x_map=lambda s, j: (s * nt + j, 0),
                     out_shapes=[(nbh * length, _BSUB * d)],
                     out_maps=[lambda s, j: ((s // _BSUB) * nt + j, s % _BSUB)])[0]
        u_tm = u_tm.reshape(nbh, length * _BSUB, d)
    else:
        u = _proj(x2, w_in, b * length, [(0, d)], [_F32])[0]
        u_tm = jnp.transpose(u.reshape(nbh, _BSUB, length, d), (0, 2, 1, 3)).reshape(nbh, length * _BSUB, d)
    g_tm, h_t = _s5_scan(u_tm, bblk, cblk, a, d_skip, _state_to_blocks(h0), tc_len)
    if length % tm == 0:
        nt = length // tm
        f = g_tm.reshape(nbh * length, _BSUB * d)
        f_map = lambda i: (((i // nt) // _BSUB) * nt + i % nt, (i // nt) % _BSUB)
    else:
        f = jnp.transpose(g_tm.reshape(nbh, length, _BSUB, d), (0, 2, 1, 3)).reshape(b * length, d)
        f_map = lambda i: (i, 0)
    w_group, w_expert, w_gate_up, w_down = moe_w
    wr = _router_weight(w_group, w_expert)
    x1, route, w1b, w2b, cnt = _mix_ln_route(f, f_map, (tm, d), x2, w_glu, ln_g[0], ln_b[0], wr, tm, tk, True)
    x2o = _moe(x1, route, w1b, w2b, cnt, w_gate_up, w_down, ln_g[1], ln_b[1], tk)
    return x2o, _blocks_to_state(h_t)


_HALF = _LANES // 2
_N_PAIR = _N_KV // 2
_KV_HALF = _N_KV * _HEAD_DIM


def _head_perm():
    idx = []
    for p in range(_N_PAIR):
        for h in range(_HPG):
            for gs in range(2):
                base = ((2 * p + gs) * _HPG + h) * _HEAD_DIM
                idx.extend(range(base, base + _HEAD_DIM))
    return np.asarray(idx, np.int32)


def _gate_col(p, h, gs, branch):
    return ((2 * p + gs) * _HPG + h) * _N_BRANCH + branch


def _nsa_weights(kv_w, cmp_pe, cmp_w1, cmp_w2, w_qg, w_o):
    nq = _N_HEADS * _HEAD_DIM
    perm = _head_perm()
    w_q = w_qg[:, :nq][:, perm] * _ATTN_SCALE
    w_g = w_qg[:, nq:]
    w_g = jnp.pad(w_g, ((0, 0), (0, _LANES - w_g.shape[1])))
    half = _CMP_STRIDE * _HEAD_DIM
    w1 = jnp.concatenate([cmp_w1[:, :half], cmp_w1[:, half:]], axis=2)
    pe = cmp_pe.reshape(2, 2, 1, half)
    pe = jnp.broadcast_to(pe, (2, 2, _SUBLANES, half))
    return dict(kv_w=kv_w, w_qg=jnp.concatenate([w_q, w_g], axis=1), w_o=w_o[perm], w1=_mx(w1), w2=_mx(cmp_w2), pe=pe)


def _sel_sum_matrix(n_rows, n_cmp, n_sel, lanes):
    m = np.zeros((n_rows, lanes), np.float32)
    for c in range(n_cmp):
        j0 = (c * _CMP_STRIDE) // _SEL_BLOCK
        j1 = (c * _CMP_STRIDE + _CMP_BLOCK - 1) // _SEL_BLOCK
        if j0 < n_sel:
            m[c, j0] += 1.0
        if j1 != j0 and j1 < n_sel:
            m[c, j1] += 1.0
    return jnp.asarray(m, _MXU_DTYPE)


def _expand_matrix(lanes, n_keys):
    j = np.arange(lanes)[:, None]
    key = np.arange(n_keys)[None, :]
    return jnp.asarray((key // _SEL_BLOCK == j).astype(np.float32), _MXU_DTYPE)


_CMP_PAGES = 16
_TB_PITCH = 24


def _compress_kernel(npg, transposed, pt_ref, *refs):
    page_refs = refs[:npg + 1]
    w1_ref, w2_ref, pe_ref, o_ref, lhs_ref, tbuf_ref = refs[npg + 1:]
    nlb = 2 * _N_PAIR
    page = page_refs[0].shape[1] if transposed else page_refs[0].shape[0]
    spp = page // _CMP_STRIDE
    seg = npg * spp + _SUBLANES
    rows = _N_KV * seg
    hd = _HEAD_DIM

    @pl.when((pl.program_id(0) == 0) & (pl.program_id(1) == 0))
    def _():
        lhs_ref[...] = jnp.zeros_like(lhs_ref)

    def lane_block(ref, c):
        if transposed:
            return jnp.transpose(ref[c * _LANES:(c + 1) * _LANES, :])
        return ref[:, c * _LANES:(c + 1) * _LANES]

    for p in range(npg + 1):
        nsb = spp if p < npg else 1
        for c in range(nlb):
            blk = lane_block(page_refs[p], c)
            for n in range(nsb):
                tbuf_ref[p, c, n * _TB_PITCH:n * _TB_PITCH + _CMP_STRIDE, :] = blk[n * _CMP_STRIDE:(n + 1) * _CMP_STRIDE]
            for s in range(0, _CMP_STRIDE, 2):
                xa = tbuf_ref[p, c, pl.ds(s, nsb, stride=_TB_PITCH), :]
                xb = tbuf_ref[p, c, pl.ds(s + 1, nsb, stride=_TB_PITCH), :]
                for half in range(2):
                    k, g = divmod(2 * c + half, _N_KV)
                    r0 = g * seg + p * spp
                    hs = slice(half * hd, (half + 1) * hd)
                    lhs_ref[k, r0:r0 + nsb, s * hd:(s + 2) * hd] = jnp.concatenate([xa[:, hs], xb[:, hs]], axis=1)

    pieces = []
    for k in range(2):
        w_lo, w_hi = w1_ref[k, :, :_CMP_HIDDEN], w1_ref[k, :, _CMP_HIDDEN:]
        lohi = _dot(lhs_ref[k, 0:rows, :], w1_ref[k])
        lo = lohi[:, :_CMP_HIDDEN]
        hi = pltpu.roll(lohi[:, _CMP_HIDDEN:], rows - 1, axis=0)
        pe = _dot(pe_ref[k, 0], w_lo)[0:1] + _dot(pe_ref[k, 1], w_hi)[0:1]
        out = _dot(jax.nn.gelu(lo + hi + pe), w2_ref[k])
        pieces += [out[g * seg:g * seg + npg * spp] for g in range(_N_KV)]
    o_ref[...] = jnp.concatenate(pieces, axis=1).astype(o_ref.dtype)


def _compress(pages, table, nsa, npg, transposed):
    b, npp = table.shape
    page, width = (pages.shape[2], pages.shape[1]) if transposed else pages.shape[1:]
    nlb = width // _LANES
    spp = page // _CMP_STRIDE
    seg = npg * spp + _SUBLANES
    half = _CMP_STRIDE * _HEAD_DIM

    def page_map(i):
        return lambda s, j, pt: (pt[s * npp + jnp.minimum(j * npg + i, npp - 1)], 0, 0)

    const3 = lambda s, j, pt: (0, 0, 0)
    const4 = lambda s, j, pt: (0, 0, 0, 0)
    grid_spec = pltpu.PrefetchScalarGridSpec(
        num_scalar_prefetch=1,
        grid=(b, npp // npg),
        in_specs=[pl.BlockSpec((None,) + pages.shape[1:], page_map(i)) for i in range(npg + 1)]
        + [pl.BlockSpec((2, half, 2 * _CMP_HIDDEN), const3), pl.BlockSpec((2, _CMP_HIDDEN, _HEAD_DIM), const3),
           pl.BlockSpec((2, 2, _SUBLANES, half), const4)],
        out_specs=pl.BlockSpec((None, npg * spp, width), lambda s, j, pt: (s, j, 0)),
        scratch_shapes=[pltpu.VMEM((2, _N_KV * seg + _SUBLANES, half), _F32),
                        pltpu.VMEM((npg + 1, nlb, spp * _TB_PITCH, _LANES), _F32)],
    )
    return pl.pallas_call(
        functools.partial(_compress_kernel, npg, transposed),
        grid_spec=grid_spec,
        out_shape=jax.ShapeDtypeStruct((b, npp * spp, width), _MXU_DTYPE),
        compiler_params=_params("arbitrary", "arbitrary"),
        name="nsa_compress",
    )(table.reshape(-1), *([pages] * (npg + 1)), nsa["w1"], nsa["w2"], nsa["pe"])


def _softmax_parts(s, mask):
    sm = jnp.where(mask, s, _NEG_INF)
    m = jnp.max(sm, -1, keepdims=True)
    e = jnp.where(mask, jnp.exp(sm - m), 0.0)
    return e, jnp.maximum(jnp.sum(e, -1, keepdims=True), _TINY)


def _dot_exact_rhs(a, b):
    a1 = _mx(a)
    r1 = a - a1.astype(_F32)
    a2 = _mx(r1)
    a3 = _mx(r1 - a2.astype(_F32))
    return _dot(a1, b) + (_dot(a2, b) + _dot(a3, b))


def _top_k_mask(score, k):
    lane = lax.broadcasted_iota(jnp.int32, score.shape, 1).astype(_F32)
    sel = jnp.zeros(score.shape, _F32)
    for _ in range(k):
        m = jnp.max(score, -1, keepdims=True)
        idx = jnp.min(jnp.where(score == m, lane, float(score.shape[1])), -1, keepdims=True)
        hit = lane == idx
        sel = jnp.where(hit, 1.0, sel)
        score = jnp.where(hit, -jnp.inf, score)
    return sel


def _select_mask(pg, mt_ref, q_pos, n_sel, k_sel):
    ps = _dot_exact_rhs(pg, mt_ref[...])
    j = lax.broadcasted_iota(jnp.int32, ps.shape, 1)
    valid = j * _SEL_BLOCK <= q_pos
    own = jnp.right_shift(q_pos, int(math.log2(_SEL_BLOCK)))
    forced = jnp.where(j == own, 1.0, jnp.where(j == 0, 1.0, 0.0))
    score = jnp.where(valid, ps + _FORCE_SCORE * forced, -1.0)
    score = jnp.where(j < n_sel, score, -jnp.inf)
    return _top_k_mask(score, k_sel)


def _masked_halves(blk):
    low = lax.broadcasted_iota(jnp.int32, blk.shape, 1) < _HALF
    return jnp.concatenate([jnp.where(low, blk, 0.0), jnp.where(low, 0.0, blk)], axis=0)


def _merge_halves(o, rows):
    low = lax.broadcasted_iota(jnp.int32, (rows, _LANES), 1) < _HALF
    return jnp.where(low, o[:rows], o[rows:])


def _gate_block(gates, p, h, branch):
    low = lax.broadcasted_iota(jnp.int32, (gates.shape[0], _LANES), 1) < _HALF
    c0, c1 = _gate_col(p, h, 0, branch), _gate_col(p, h, 1, branch)
    return jnp.where(low, gates[:, c0:c0 + 1], gates[:, c1:c1 + 1])


_KEY_CHUNK = 512


def _biased_attend(lq, k, v, bias):
    nk = k.shape[0]
    parts = [(c0, min(c0 + _KEY_CHUNK, nk)) for c0 in range(0, nk, _KEY_CHUNK)]
    s = [_dot_nt(lq, k[c0:c1]) + bias[:, c0:c1] for c0, c1 in parts]
    m = functools.reduce(jnp.maximum, [jnp.max(sc, -1, keepdims=True) for sc in s])
    den, acc = 0.0, 0.0
    for sc, (c0, c1) in zip(s, parts):
        e = jnp.exp(sc - m)
        den = den + jnp.sum(e, -1, keepdims=True)
        acc = acc + _dot(e, v[c0:c1])
    return acc * (1.0 / jnp.maximum(den, _TINY))


def _rank_select(score, n_sel, k_sel):
    st = jnp.transpose(score)
    top = -(-n_sel // _SUBLANES) * _SUBLANES
    s_top = st[:top]
    ridx = lax.broadcasted_iota(jnp.int32, s_top.shape, 0)
    ahead = jnp.zeros(s_top.shape, _F32)
    for j in range(n_sel):
        sj = st[j:j + 1, :]
        ahead = ahead + jnp.where(sj > s_top, 1.0, 0.0) + jnp.where(sj == s_top, jnp.where(ridx > j, 1.0, 0.0), 0.0)
    sel_t = jnp.where(ahead < k_sel, 1.0, 0.0)
    sel_t = jnp.concatenate([sel_t, jnp.zeros((st.shape[0] - top, st.shape[1]), _F32)], axis=0)
    return jnp.transpose(sel_t)


def _attn_prompt_kernel(tq, tile0, n_sel, k_sel, q_ref, gate_ref, ckv_ref, slc_ref, win_ref, mt_ref, ex_ref, o_ref):
    t0 = (tile0 + pl.program_id(1)) * tq
    nk = slc_ref.shape[0]
    nc = ckv_ref.shape[0]
    q_pos = t0 + lax.broadcasted_iota(jnp.int32, (tq, 1), 0)
    qf = q_ref[...].astype(_F32)
    gates = gate_ref[...]
    c_end = lax.broadcasted_iota(jnp.int32, (1, nc), 1) * _CMP_STRIDE + (_CMP_BLOCK - 1)
    cmask = c_end <= q_pos
    cmask2 = jnp.concatenate([cmask, cmask], axis=0)
    causal_bias = jnp.where(lax.broadcasted_iota(jnp.int32, (1, nk), 1) <= q_pos, 0.0, _NEG_INF)
    wk = min(nk, tq + _WINDOW)
    wstart = pl.multiple_of(jnp.clip(t0 - _WINDOW, 0, nk - wk), tq)
    dist = q_pos - (wstart + lax.broadcasted_iota(jnp.int32, (1, wk), 1))
    wbias = jnp.where(dist >= 0, jnp.where(dist < _WINDOW, 0.0, _NEG_INF), _NEG_INF)
    wbias2 = jnp.concatenate([wbias, wbias], axis=0)
    j = lax.broadcasted_iota(jnp.int32, (tq, _LANES), 1)
    valid = j * _SEL_BLOCK <= q_pos
    own = jnp.right_shift(q_pos, int(math.log2(_SEL_BLOCK)))
    forced = jnp.where(j == own, _FORCE_SCORE, jnp.where(j == 0, _FORCE_SCORE, 0.0))

    for p in range(_N_PAIR):
        ksl = slice(p * _LANES, (p + 1) * _LANES)
        vsl = slice(_KV_HALF + p * _LANES, _KV_HALF + (p + 1) * _LANES)
        kc, vc = ckv_ref[:, ksl], ckv_ref[:, vsl]
        lhs, o_cmp = [], []
        pg = [jnp.zeros((tq, nc), _F32), jnp.zeros((tq, nc), _F32)]
        for h in range(_HPG):
            blk = p * _HPG + h
            lq = _mx(_masked_halves(qf[:, blk * _LANES:(blk + 1) * _LANES]))
            lhs.append(lq)
            e, den = _softmax_parts(_dot_nt(lq, kc), cmask2)
            pc = e * (1.0 / den)
            o_cmp.append(_merge_halves(_dot(pc, vc), tq))
            pg = [pg[0] + pc[:tq], pg[1] + pc[tq:]]
        sbias = []
        for gs in range(2):
            ps = _dot_exact_rhs(pg[gs], mt_ref[...])
            score = jnp.where(valid, ps + forced, -1.0)
            score = jnp.where(j < n_sel, score, -jnp.inf)
            sel = _rank_select(score, n_sel, k_sel)
            picked = _dot(sel, ex_ref[...])
            sbias.append(jnp.minimum(causal_bias, (picked - 1.0) * (-_NEG_INF)))
        sbias2 = jnp.concatenate(sbias, axis=0)
        ks, vs = slc_ref[:, ksl], slc_ref[:, vsl]
        kw, vw = win_ref[pl.ds(wstart, wk), ksl], win_ref[pl.ds(wstart, wk), vsl]
        for h in range(_HPG):
            blk = p * _HPG + h
            o_slc = _merge_halves(_biased_attend(lhs[h], ks, vs, sbias2), tq)
            o_win = _merge_halves(_biased_attend(lhs[h], kw, vw, wbias2), tq)
            o = (_gate_block(gates, p, h, 0) * o_cmp[h] + _gate_block(gates, p, h, 1) * o_slc
                 + _gate_block(gates, p, h, 2) * o_win)
            o_ref[:, blk * _LANES:(blk + 1) * _LANES] = o.astype(o_ref.dtype)


def _nsa_prompt(x, b, length, nsa, tm):
    t, d = x.shape
    kvw = 2 * _KV_HALF
    cmp_f, slc_f, win_f, slc_b, win_b = _proj(
        x, nsa["kv_w"], tm, [(0, kvw), (kvw, 2 * kvw), (2 * kvw, 3 * kvw), (kvw, 2 * kvw), (2 * kvw, 3 * kvw)],
        [_F32, _F32, _F32, _MXU_DTYPE, _MXU_DTYPE])
    nq = _N_HEADS * _HEAD_DIM
    q, gates = _proj(x, nsa["w_qg"], tm, [(0, nq), (nq, nq + _LANES)], [_MXU_DTYPE, _F32], acts=[None, "sigmoid"])
    page = _LANES
    npp = length // page
    table = jnp.arange(b * npp, dtype=jnp.int32).reshape(b, npp)
    ckv = _compress(cmp_f.reshape(b * npp, page, kvw), table, nsa, min(_CMP_PAGES, npp), False)
    nc = length // _CMP_STRIDE
    tq = _LANES
    nq_t = length // tq
    mt = _sel_sum_matrix(nc, nc - 1, -(-length // _SEL_BLOCK), _LANES)
    slc3, win3 = slc_b.reshape(b, length, kvw), win_b.reshape(b, length, kvw)
    seq = lambda s, i: (s, 0, 0)
    const = lambda s, i: (0, 0)
    n_buckets = 4 if nq_t % 4 == 0 else 1
    tiles = nq_t // n_buckets
    outs = []
    for bk in range(n_buckets):
        tile0 = bk * tiles
        nk = (tile0 + tiles) * tq
        n_sel = nk // _SEL_BLOCK
        nkc = nk // _CMP_STRIDE
        q_map = lambda s, i, tile0=tile0: (s * nq_t + tile0 + i, 0)
        outs.append(pl.pallas_call(
            functools.partial(_attn_prompt_kernel, tq, tile0, n_sel, min(_N_SELECT, n_sel)),
            grid=(b, tiles),
            in_specs=[pl.BlockSpec((tq, nq), q_map), pl.BlockSpec((tq, _LANES), q_map),
                      pl.BlockSpec((None, nkc, kvw), seq), pl.BlockSpec((None, nk, kvw), seq),
                      pl.BlockSpec((None, nk, kvw), seq),
                      pl.BlockSpec((nkc, _LANES), const), pl.BlockSpec((_LANES, nk), const)],
            out_specs=pl.BlockSpec((None, tq, nq), lambda s, i: (s, i, 0)),
            out_shape=jax.ShapeDtypeStruct((b, tiles * tq, nq), _MXU_DTYPE),
            compiler_params=_params("arbitrary", "arbitrary"),
            name=f"nsa_prompt_attn_{bk}",
        )(q, gates, ckv, slc3, win3, mt[:nkc], _expand_matrix(_LANES, nk)))
    o = jnp.concatenate(outs, axis=1).reshape(t, nq)
    return o, cmp_f, slc_f, win_f


def _attn_sample_cmp_kernel(n_sel, k_sel, past, q_ref, ckv_ref, mt_ref, ocmp_ref, sel_ref):
    s_new = q_ref.shape[0]
    nc = ckv_ref.shape[0]
    q_pos = past + lax.broadcasted_iota(jnp.int32, (s_new, 1), 0)
    qf = q_ref[...].astype(_F32)
    c_end = lax.broadcasted_iota(jnp.int32, (1, nc), 1) * _CMP_STRIDE + (_CMP_BLOCK - 1)
    cmask = c_end <= q_pos
    cmask2 = jnp.concatenate([cmask, cmask], axis=0)
    pg_all = []
    for p in range(_N_PAIR):
        kc = ckv_ref[:, p * _LANES:(p + 1) * _LANES]
        vc = ckv_ref[:, _KV_HALF + p * _LANES:_KV_HALF + (p + 1) * _LANES]
        pg = [jnp.zeros((s_new, nc), _F32), jnp.zeros((s_new, nc), _F32)]
        for h in range(_HPG):
            blk = p * _HPG + h
            lq = _masked_halves(qf[:, blk * _LANES:(blk + 1) * _LANES])
            e, den = _softmax_parts(_dot_nt(lq, kc), cmask2)
            pc = e / den
            ocmp_ref[:, blk * _LANES:(blk + 1) * _LANES] = _merge_halves(_dot(pc, vc), s_new)
            pg = [pg[0] + pc[:s_new], pg[1] + pc[s_new:]]
        pg_all += pg
    sel = _select_mask(jnp.concatenate(pg_all, axis=0), mt_ref, jnp.tile(q_pos, (_N_KV, 1)), n_sel, k_sel)
    for g in range(_N_KV):
        sel_ref[g] = sel[g * s_new:(g + 1) * s_new]


def _attn_sample_kernel(npg, past, pt_ref, *refs):
    page_refs = refs[:npg]
    (q_ref, gate_ref, ocmp_ref, selc_ref, seln_ref, slcn_ref, winc_ref, winn_ref, ex_ref,
     o_ref, m_ref, l_ref, acc_ref) = refs[npg:]
    j = pl.program_id(1)
    s_new = q_ref.shape[0]
    rows = 2 * _HPG * s_new
    qf = q_ref[...].astype(_F32)

    @pl.when(j == 0)
    def _():
        m_ref[...] = jnp.full(m_ref.shape, _NEG_INF, _F32)
        l_ref[...] = jnp.zeros_like(l_ref)
        acc_ref[...] = jnp.zeros_like(acc_ref)

    def pair_lhs(p):
        halves = [_masked_halves(qf[:, (p * _HPG + h) * _LANES:(p * _HPG + h + 1) * _LANES]) for h in range(_HPG)]
        return jnp.concatenate([hv[:s_new] for hv in halves] + [hv[s_new:] for hv in halves], axis=0)

    def online(p, s, mask, pv):
        sm = jnp.where(mask, s, _NEG_INF)
        m_old = m_ref[p]
        m_new = jnp.maximum(m_old, jnp.max(sm, -1, keepdims=True))
        alpha = jnp.exp(m_old - m_new)
        e = jnp.where(mask, jnp.exp(sm - m_new), 0.0)
        l_ref[p] = alpha * l_ref[p] + jnp.sum(e, -1, keepdims=True)
        acc_ref[p] = alpha * acc_ref[p] + pv(e)
        m_ref[p] = m_new

    for p in range(_N_PAIR):
        ksl = slice(p * _LANES, (p + 1) * _LANES)
        vsl = slice(_KV_HALF + p * _LANES, _KV_HALF + (p + 1) * _LANES)
        lq = pair_lhs(p)
        k_t = jnp.concatenate([r[ksl, :] for r in page_refs], axis=1)
        v_t = jnp.concatenate([r[vsl, :] for r in page_refs], axis=1)
        mask = jnp.concatenate(
            [jnp.tile(_dot(selc_ref[2 * p + gs], ex_ref[...]) > 0.5, (_HPG, 1)) for gs in range(2)], axis=0)
        online(p, _dot(lq, k_t), mask, lambda e, v_t=v_t: _dot_nt(e, v_t))

    @pl.when(j == pl.num_programs(1) - 1)
    def _():
        gates = gate_ref[...]
        tok = lax.broadcasted_iota(jnp.int32, (s_new, 1), 0)
        q_pos = past + jnp.tile(tok, (2 * _HPG, 1))
        newer = lax.broadcasted_iota(jnp.int32, (1, s_new), 1) <= jnp.tile(tok, (2 * _HPG, 1))
        n_cache = winc_ref.shape[1]
        k_pos = past - n_cache + lax.broadcasted_iota(jnp.int32, (1, n_cache + s_new), 1)
        dist = q_pos - k_pos
        wmask = (dist >= 0) & (dist < _WINDOW) & (k_pos >= 0)
        wmask_c, wmask_n = wmask[:, :n_cache], wmask[:, n_cache:]
        for p in range(_N_PAIR):
            ksl = slice(p * _LANES, (p + 1) * _LANES)
            vsl = slice(_KV_HALF + p * _LANES, _KV_HALF + (p + 1) * _LANES)
            lq = pair_lhs(p)
            nmask = jnp.concatenate(
                [jnp.tile(seln_ref[2 * p + gs][:, :s_new] > 0.5, (_HPG, 1)) for gs in range(2)], axis=0) & newer
            online(p, _dot_nt(lq, slcn_ref[:, ksl]), nmask, lambda e, vsl=vsl: _dot(e, slcn_ref[:, vsl]))
            o_slc = acc_ref[p] / jnp.maximum(l_ref[p], _TINY)
            s_c = jnp.where(wmask_c, _dot(lq, winc_ref[ksl, :]), _NEG_INF)
            s_n = jnp.where(wmask_n, _dot_nt(lq, winn_ref[:, ksl]), _NEG_INF)
            m = jnp.maximum(jnp.max(s_c, -1, keepdims=True), jnp.max(s_n, -1, keepdims=True))
            e_c = jnp.where(wmask_c, jnp.exp(s_c - m), 0.0)
            e_n = jnp.where(wmask_n, jnp.exp(s_n - m), 0.0)
            den = jnp.maximum(jnp.sum(e_c, -1, keepdims=True) + jnp.sum(e_n, -1, keepdims=True), _TINY)
            o_win = (_dot_nt(e_c, winc_ref[vsl, :]) + _dot(e_n, winn_ref[:, vsl])) / den
            for h in range(_HPG):
                blk = p * _HPG + h
                lo, hi = h * s_new, (_HPG + h) * s_new
                pick = lambda o: _merge_halves(jnp.concatenate([o[lo:lo + s_new], o[hi:hi + s_new]], axis=0), s_new)
                o = (_gate_block(gates, p, h, 0) * ocmp_ref[:, blk * _LANES:(blk + 1) * _LANES]
                     + _gate_block(gates, p, h, 1) * pick(o_slc) + _gate_block(gates, p, h, 2) * pick(o_win))
                o_ref[:, blk * _LANES:(blk + 1) * _LANES] = o


def _nsa_sample(x, b, s_new, nsa, cache_cmp_kv, cache_slc_kv, cache_win_kv, page_table):
    t, d = x.shape
    kvw = 2 * _KV_HALF
    nq = _N_HEADS * _HEAD_DIM
    n_pool, page = cache_cmp_kv.shape[:2]
    npp = page_table.shape[1]
    past = npp * page
    assert (past + s_new) // _CMP_STRIDE == past // _CMP_STRIDE and past % _SEL_BLOCK == 0
    cmp_f, slc_f, win_f = _proj(x, nsa["kv_w"], t, [(0, kvw), (kvw, 2 * kvw), (2 * kvw, 3 * kvw)], [_F32] * 3)
    q, gates = _proj(x, nsa["w_qg"], t, [(0, nq), (nq, nq + _LANES)], [_F32, _F32], acts=[None, "sigmoid"])
    kv_t = lambda c: jnp.transpose(c, (0, 2, 3, 4, 1)).reshape(c.shape[0], kvw, c.shape[1])
    ckv = _compress(kv_t(cache_cmp_kv), page_table, nsa, min(_CMP_PAGES, npp), True)
    nc = past // _CMP_STRIDE
    n_sel = -(-(past + s_new) // _SEL_BLOCK)
    sel_lanes = -(-n_sel // _LANES) * _LANES
    mt = _sel_sum_matrix(nc, nc - 1, n_sel, sel_lanes)
    seq2 = lambda s: (s, 0)
    o_cmp, sel = pl.pallas_call(
        functools.partial(_attn_sample_cmp_kernel, n_sel, min(_N_SELECT, n_sel), past),
        grid=(b,),
        in_specs=[pl.BlockSpec((s_new, nq), seq2), pl.BlockSpec((None, nc, kvw), lambda s: (s, 0, 0)),
                  pl.BlockSpec((nc, sel_lanes), lambda s: (0, 0))],
        out_specs=[pl.BlockSpec((s_new, nq), seq2),
                   pl.BlockSpec((None, _N_KV, s_new, sel_lanes), lambda s: (s, 0, 0, 0))],
        out_shape=[jax.ShapeDtypeStruct((t, nq), _F32), jax.ShapeDtypeStruct((b, _N_KV, s_new, sel_lanes), _F32)],
        compiler_params=_params("arbitrary"),
    )(q, ckv, mt)

    npg = min(16, npp)
    n_chunks = npp // npg
    bpc = npg * page // _SEL_BLOCK
    n_past_blocks = past // _SEL_BLOCK
    selc = sel[..., :n_past_blocks].reshape(b, _N_KV, s_new, n_chunks, bpc)
    selc = jnp.pad(jnp.transpose(selc, (0, 3, 1, 2, 4)), ((0, 0),) * 4 + ((0, _LANES - bpc),))
    seln = jnp.broadcast_to(sel[..., n_past_blocks:n_past_blocks + 1], (b, _N_KV, s_new, _LANES))
    ex = _expand_matrix(_LANES, npg * page)
    n_cache = cache_win_kv.shape[1]

    def page_map(i):
        return lambda s, j, pt: (pt[s * npp + j * npg + i], 0, 0)

    seq = lambda s, j, pt: (s, 0)
    seq3 = lambda s, j, pt: (s, 0, 0)
    grid_spec = pltpu.PrefetchScalarGridSpec(
        num_scalar_prefetch=1,
        grid=(b, n_chunks),
        in_specs=[pl.BlockSpec((None, kvw, page), page_map(i)) for i in range(npg)]
        + [pl.BlockSpec((s_new, nq), seq), pl.BlockSpec((s_new, _LANES), seq), pl.BlockSpec((s_new, nq), seq),
           pl.BlockSpec((None, None, _N_KV, s_new, _LANES), lambda s, j, pt: (s, j, 0, 0, 0)),
           pl.BlockSpec((None, _N_KV, s_new, _LANES), lambda s, j, pt: (s, 0, 0, 0)),
           pl.BlockSpec((s_new, kvw), seq), pl.BlockSpec((None, kvw, n_cache), seq3), pl.BlockSpec((s_new, kvw), seq),
           pl.BlockSpec((_LANES, npg * page), lambda s, j, pt: (0, 0))],
        out_specs=pl.BlockSpec((s_new, nq), seq),
        scratch_shapes=[pltpu.VMEM((_N_PAIR, 2 * _HPG * s_new, 1), _F32), pltpu.VMEM((_N_PAIR, 2 * _HPG * s_new, 1), _F32),
                        pltpu.VMEM((_N_PAIR, 2 * _HPG * s_new, _LANES), _F32)],
    )
    o = pl.pallas_call(
        functools.partial(_attn_sample_kernel, npg, past),
        grid_spec=grid_spec,
        out_shape=jax.ShapeDtypeStruct((t, nq), _F32),
        compiler_params=_params("arbitrary", "arbitrary"),
        name="nsa_sample_attn",
    )(page_table.reshape(-1), *([kv_t(cache_slc_kv)] * npg), q, gates, o_cmp, selc, seln,
      slc_f, kv_t(cache_win_kv), win_f, ex)
    return o, cmp_f, slc_f, win_f


def kernel(x_prompt, x_sample, state_ssm, cache_cmp_kv, cache_slc_kv, cache_win_kv, page_table, a_w_in, a_lambda_re, a_lambda_im, a_log_dt, a_b_re, a_b_im, a_c_re, a_c_im, a_d, a_w_glu, kv_w, cmp_pe, cmp_w1, cmp_w2, b_w_qg, b_w_o, moe_w_group, moe_w_expert, moe_w_gate_up, moe_w_down, ln_g, ln_b):
    d = x_prompt.shape[-1]
    bp, lp, _ = x_prompt.shape
    bs, ls, _ = x_sample.shape
    tm_p, tk_p, tc_p = min(512, lp), min(2048, bp * lp), min(64, lp)
    ts = bs * ls

    ar, ai, bbr, bbi = _s5_discretize(a_lambda_re[0], a_lambda_im[0], a_log_dt[0], a_b_re[0], a_b_im[0])
    s5w = _s5_block_weights(ar, ai, bbr, bbi, a_c_re[0], a_c_im[0])
    moe0 = (moe_w_group[0], moe_w_expert[0], moe_w_gate_up[0], moe_w_down[0])
    h0p = jnp.zeros((bp,) + state_ssm.shape[2:], _F32)
    xp, hp = _layer0(x_prompt, h0p, s5w, a_w_in[0], a_d[0], a_w_glu[0], ln_g[0], ln_b[0], moe0, tm_p, tk_p, tc_p)
    xs, hs = _layer0(x_sample, state_ssm[0], s5w, a_w_in[0], a_d[0], a_w_glu[0], ln_g[0], ln_b[0], moe0, ts, ts, ls)

    nsa = _nsa_weights(kv_w, cmp_pe, cmp_w1, cmp_w2, b_w_qg[0], b_w_o[0])
    moe1 = (moe_w_group[1], moe_w_expert[1], moe_w_gate_up[1], moe_w_down[1])
    wr1 = _router_weight(moe1[0], moe1[1])
    kvw = 2 * _N_KV * _HEAD_DIM
    kv_shape = (2, _N_KV, _HEAD_DIM)

    op, cmp_p, slc_p, win_p = _nsa_prompt(xp, bp, lp, nsa, tm_p)
    x1p, route, w1b, w2b, cnt = _mix_ln_route(op, lambda i: (i, 0), (tm_p, d), xp, nsa["w_o"], ln_g[1, 0], ln_b[1, 0],
                                             wr1, tm_p, tk_p, False)
    yp = _moe(x1p, route, w1b, w2b, cnt, moe1[2], moe1[3], ln_g[1, 1], ln_b[1, 1], tk_p)

    os_, cmp_s, slc_s, win_s = _nsa_sample(xs, bs, ls, nsa, cache_cmp_kv, cache_slc_kv, cache_win_kv, page_table)
    x1s, route, w1b, w2b, cnt = _mix_ln_route(os_, lambda i: (i, 0), (ts, d), xs, nsa["w_o"], ln_g[1, 0], ln_b[1, 0],
                                             wr1, ts, ts, False)
    ys = _moe(x1s, route, w1b, w2b, cnt, moe1[2], moe1[3], ln_g[1, 1], ln_b[1, 1], ts)

    n_keep = min(_WINDOW, lp)
    p_win = win_p.reshape((bp, lp) + kv_shape)[:, lp - n_keep:]
    win_all = jnp.concatenate([cache_win_kv, win_s.reshape((bs, ls) + kv_shape).astype(cache_win_kv.dtype)], axis=1)
    s_win = win_all[:, win_all.shape[1] - min(_WINDOW, win_all.shape[1]):]
    return (yp.reshape(bp, lp, d), ys.reshape(bs, ls, d), hp[None],
            cmp_p.reshape((bp, lp) + kv_shape), slc_p.reshape((bp, lp) + kv_shape), p_win,
            hs[None].astype(state_ssm.dtype),
            cmp_s.reshape((bs, ls) + kv_shape), slc_s.reshape((bs, ls) + kv_shape), s_win)
```

```python
import functools
import math

import jax
import jax.numpy as jnp
import numpy as np
from jax import lax
from jax.experimental import pallas as pl
from jax.experimental.pallas import tpu as pltpu

_F32 = jnp.float32
_MXU_DTYPE = jnp.bfloat16
_LANES = 128
_SUBLANES = 8
_VMEM_LIMIT = 56 * 1024 * 1024

_DEPTH = 2
_SSM_W = 16
_SSM_P = 64
_S5_LANES = _LANES
_GB = _S5_LANES // _SSM_W
_BSUB = 16
_N_HEADS = 16
_HEAD_DIM = 64
_N_KV = 4
_HPG = _N_HEADS // _N_KV
_N_BRANCH = 3
_CMP_BLOCK = 32
_CMP_STRIDE = 16
_CMP_HIDDEN = 256
_SEL_BLOCK = 64
_N_SELECT = 16
_WINDOW = 512
_ATTN_SCALE = _HEAD_DIM ** -0.5
_FORCE_SCORE = 1e4
_NEG_INF = -1e30
_TINY = 1e-30
_NG = 4
_EPG = 8
_NE = _NG * _EPG
_D_EXPERT = 256
_MOE_ROWS = 160
_ALPHA = (2 * _DEPTH) ** 0.25
_LN_EPS = 1e-5


def _mx(x):
    return x.astype(_MXU_DTYPE)


def _dot(a, b):
    return jnp.dot(_mx(a), _mx(b), preferred_element_type=_F32)


def _dot_nt(a, b):
    return lax.dot_general(_mx(a), _mx(b), (((1,), (1,)), ((), ())), preferred_element_type=_F32)


def _split2(a):
    a1 = _mx(a)
    return a1, _mx(a - a1.astype(_F32))


def _dot_hi(a, b):
    a1, a2 = _split2(a)
    b1, b2 = _split2(b)
    return _dot(a1, b1) + (_dot(a1, b2) + _dot(a2, b1))


def _params(*sem):
    return pltpu.CompilerParams(dimension_semantics=sem, vmem_limit_bytes=_VMEM_LIMIT)


def _layer_norm(v, g, b):
    mu = jnp.mean(v, -1, keepdims=True)
    d = v - mu
    var = jnp.mean(d * d, -1, keepdims=True)
    return d * lax.rsqrt(var + _LN_EPS) * g + b


def _proj_kernel(cols, acts, x_ref, w_ref, *o_refs):
    y = _dot(x_ref[...], w_ref[...])
    for (c0, c1), act, o_ref in zip(cols, acts, o_refs):
        v = y[:, c0:c1]
        if act == "sigmoid":
            v = jax.nn.sigmoid(v)
        o_ref[...] = v.astype(o_ref.dtype)


def _proj(x, w, tm, cols, dtypes, acts=None, grid=None, x_map=None, out_shapes=None, out_maps=None):
    m, k = x.shape
    n = w.shape[1]
    acts = acts or [None] * len(cols)
    if grid is None:
        grid = (m // tm,)
        x_map = lambda i: (i, 0)
        out_shapes = [(m, c1 - c0) for c0, c1 in cols]
        out_maps = [x_map] * len(cols)
    w_map = (lambda *a: (0, 0))
    return pl.pallas_call(
        functools.partial(_proj_kernel, tuple(cols), tuple(acts)),
        grid=grid,
        in_specs=[pl.BlockSpec((tm, k), x_map), pl.BlockSpec((k, n), w_map)],
        out_specs=[pl.BlockSpec((tm, c1 - c0), om) for (c0, c1), om in zip(cols, out_maps)],
        out_shape=[jax.ShapeDtypeStruct(s, d) for s, d in zip(out_shapes, dtypes)],
        compiler_params=_params(*(["arbitrary"] * len(grid))),
        name="proj",
    )(x, _mx(w))


def _s5_disc_kernel(lr_ref, li_ref, ldt_ref, bre_ref, bim_ref, ar_ref, ai_ref, bbr_ref, bbi_ref):
    lr, li = lr_ref[...], li_ref[...]
    dt = jnp.exp(ldt_ref[...])
    mag = jnp.exp(lr * dt)
    a_re, a_im = mag * jnp.cos(li * dt), mag * jnp.sin(li * dt)
    den = lr * lr + li * li
    nr = a_re - 1.0
    z_re = (nr * lr + a_im * li) / den
    z_im = (a_im * lr - nr * li) / den
    b_re, b_im = bre_ref[...], bim_ref[...]
    ar_ref[...] = a_re
    ai_ref[...] = a_im
    bbr_ref[...] = z_re * b_re - z_im * b_im
    bbi_ref[...] = z_re * b_im + z_im * b_re


def _s5_discretize(lam_re, lam_im, log_dt, b_re, b_im):
    g, p = lam_re.shape
    w = b_re.shape[-1]
    n = g * p
    flat = lambda a: a.reshape(1, n)
    b_t = lambda a: a.reshape(n, w).T
    ldt = jnp.repeat(log_dt, p).reshape(1, n)
    outs = pl.pallas_call(
        _s5_disc_kernel,
        out_shape=[jax.ShapeDtypeStruct((1, n), _F32)] * 2 + [jax.ShapeDtypeStruct((w, n), _F32)] * 2,
    )(flat(lam_re), flat(lam_im), ldt, b_t(b_re), b_t(b_im))
    return outs


def _s5_block_weights(abar_re, abar_im, bbt_re, bbt_im, c_re, c_im):
    g = abar_re.shape[1] // _SSM_P
    nb = g // _GB
    eye = jnp.eye(_GB, dtype=_F32)

    def in_mat(bt):
        b4 = bt.reshape(_SSM_W, nb, _GB, _SSM_P)
        return jnp.einsum("wbgp,gh->bgwhp", b4, eye).reshape(nb, _GB * _SSM_W, _GB * _SSM_P)

    def out_mat(c):
        c4 = c.reshape(nb, _GB, _SSM_W, _SSM_P)
        return jnp.einsum("bgwp,gh->bgphw", c4, eye).reshape(nb, _GB * _SSM_P, _GB * _SSM_W)

    bblk = jnp.concatenate([in_mat(bbt_re), in_mat(bbt_im)], axis=2)
    cblk = jnp.concatenate([out_mat(c_re), -out_mat(c_im)], axis=1)
    a = jnp.concatenate([abar_re.reshape(nb, 1, _GB * _SSM_P), abar_im.reshape(nb, 1, _GB * _SSM_P)], axis=2)
    a = jnp.broadcast_to(a, (nb, _BSUB, 2 * _GB * _SSM_P))
    return _mx(bblk), _mx(cblk), a


def _state_to_blocks(h):
    b, g, p, _ = h.shape
    h6 = h.reshape(b // _BSUB, _BSUB, g // _GB, _GB, p, 2)
    return jnp.transpose(h6, (0, 2, 1, 5, 3, 4)).reshape(b // _BSUB, g // _GB, _BSUB, 2 * _GB * p)


def _blocks_to_state(hb):
    nbh, nb, _, _ = hb.shape
    h6 = hb.reshape(nbh, nb, _BSUB, 2, _GB, _SSM_P)
    return jnp.transpose(h6, (0, 2, 1, 4, 5, 3)).reshape(nbh * _BSUB, nb * _GB, _SSM_P, 2)


def _s5_scan_kernel(tc_len, u_ref, bblk_ref, cblk_ref, a_ref, d_ref, h0_ref, g_ref, ht_ref, hbuf, st):
    tci = pl.program_id(2)
    half = _GB * _SSM_P

    @pl.when(tci == 0)
    def _():
        st[...] = h0_ref[...]

    u = u_ref[...]
    hbuf[...] = _dot(u, bblk_ref[...])
    a_re, a_im = a_ref[:, :half], a_ref[:, half:]

    def step(t, carry):
        h_re, h_im = carry
        r0 = pl.multiple_of(t * _BSUB, _BSUB)
        n_re = a_re * h_re - a_im * h_im + hbuf[pl.ds(r0, _BSUB), :half]
        n_im = a_re * h_im + a_im * h_re + hbuf[pl.ds(r0, _BSUB), half:]
        hbuf[pl.ds(r0, _BSUB), :half] = n_re
        hbuf[pl.ds(r0, _BSUB), half:] = n_im
        return n_re, n_im

    h_re, h_im = lax.fori_loop(0, tc_len, step, (st[:, :half], st[:, half:]))
    st[:, :half] = h_re
    st[:, half:] = h_im
    y = _dot(hbuf[...], cblk_ref[...]) + d_ref[...] * u
    g_ref[...] = jax.nn.gelu(y).astype(g_ref.dtype)

    @pl.when(tci == pl.num_programs(2) - 1)
    def _():
        ht_ref[...] = st[...]


def _s5_scan(u_tm, bblk, cblk, a, d_skip, h0_blocks, tc_len):
    nbh, rows, d = u_tm.shape
    nb = d // _S5_LANES
    length = rows // _BSUB
    ntc = length // tc_len
    tr = tc_len * _BSUB
    nstate = 2 * _GB * _SSM_P
    return pl.pallas_call(
        functools.partial(_s5_scan_kernel, tc_len),
        grid=(nbh, nb, ntc),
        in_specs=[
            pl.BlockSpec((None, tr, _S5_LANES), lambda h, b, t: (h, t, b)),
            pl.BlockSpec((None, _S5_LANES, nstate), lambda h, b, t: (b, 0, 0)),
            pl.BlockSpec((None, nstate, _S5_LANES), lambda h, b, t: (b, 0, 0)),
            pl.BlockSpec((None, _BSUB, nstate), lambda h, b, t: (b, 0, 0)),
            pl.BlockSpec((1, _S5_LANES), lambda h, b, t: (0, b)),
            pl.BlockSpec((None, None, _BSUB, nstate), lambda h, b, t: (h, b, 0, 0)),
        ],
        out_specs=[
            pl.BlockSpec((None, tr, _S5_LANES), lambda h, b, t: (h, t, b)),
            pl.BlockSpec((None, None, _BSUB, nstate), lambda h, b, t: (h, b, 0, 0)),
        ],
        out_shape=[jax.ShapeDtypeStruct((nbh, rows, d), _MXU_DTYPE),
                   jax.ShapeDtypeStruct((nbh, nb, _BSUB, nstate), _F32)],
        scratch_shapes=[pltpu.VMEM((tr, nstate), _F32), pltpu.VMEM((_BSUB, nstate), _F32)],
        compiler_params=_params("arbitrary", "arbitrary", "arbitrary"),
        name="s5_scan",
    )(u_tm, bblk, cblk, a, d_skip.reshape(1, d), h0_blocks)


def _route(xn, wr_ref, first_tile, carry_ref, route_ref, w1b_ref, w2b_ref, cnt_ref):
    tm = xn.shape[0]
    logits = _dot_hi(xn, wr_ref[...])
    lane = lax.broadcasted_iota(jnp.int32, logits.shape, 1).astype(_F32)
    neg, big = -jnp.inf, float(_LANES)
    lg = jnp.where(lane < _NG, logits, neg)
    g_top = jnp.max(lg, -1, keepdims=True)
    g_sel = jnp.min(jnp.where(lg == g_top, lane, big), -1, keepdims=True)
    lse = jnp.log(jnp.sum(jnp.exp(lg - g_top), -1, keepdims=True)) + g_top
    g_w = jnp.exp(g_top - lse)
    lo = _NG + g_sel * _EPG
    le = jnp.where((lane >= lo) & (lane < lo + _EPG), logits, neg)
    m1 = jnp.max(le, -1, keepdims=True)
    i1 = jnp.min(jnp.where(le == m1, lane, big), -1, keepdims=True)
    le2 = jnp.where(lane == i1, neg, le)
    m2 = jnp.max(le2, -1, keepdims=True)
    i2 = jnp.min(jnp.where(le2 == m2, lane, big), -1, keepdims=True)
    ex = jnp.exp(m2 - m1)
    w1 = g_w * (1.0 / (1.0 + ex))
    w2 = g_w * (ex / (1.0 + ex))

    @pl.when(first_tile)
    def _():
        carry_ref[...] = jnp.zeros_like(carry_ref)

    hit1, hit2 = lane == i1, lane == i2
    onehot = jnp.where(hit1 | hit2, 1.0, 0.0)
    r = lax.broadcasted_iota(jnp.int32, (tm, tm), 0)
    c = lax.broadcasted_iota(jnp.int32, (tm, tm), 1)
    tri = jnp.where(c < r, 1.0, 0.0)
    pref = _dot(tri, onehot) + carry_ref[...]
    rank1 = jnp.sum(jnp.where(hit1, pref, 0.0), -1, keepdims=True)
    rank2 = jnp.sum(jnp.where(hit2, pref, 0.0), -1, keepdims=True)
    carry = carry_ref[...] + jnp.sum(onehot, 0, keepdims=True)
    carry_ref[...] = carry
    cnt_ref[...] = jnp.broadcast_to(carry, cnt_ref.shape)
    slab = jnp.where(lane == 0, i1 - _NG, jnp.where(lane == 1, i2 - _NG,
                     jnp.where(lane == 2, rank1, jnp.where(lane == 3, rank2, 0.0))))
    route_ref[...] = slab[:, :_SUBLANES]
    w1b_ref[...] = jnp.broadcast_to(w1, w1b_ref.shape)
    w2b_ref[...] = jnp.broadcast_to(w2, w2b_ref.shape)


def _route_specs(tm, tpc, tile_of):
    def tile_map(*ids):
        return (tile_of(*ids), 0)

    def chunk_map(*ids):
        return (tile_of(*ids) // tpc, 0, 0)

    specs = [pl.BlockSpec((tm, _SUBLANES), tile_map), pl.BlockSpec((tm, _LANES), tile_map),
             pl.BlockSpec((tm, _LANES), tile_map), pl.BlockSpec((None, _SUBLANES, _LANES), chunk_map)]
    return specs


def _route_shapes(t, n_chunks):
    return [jax.ShapeDtypeStruct((t, _SUBLANES), _F32), jax.ShapeDtypeStruct((t, _LANES), _F32),
            jax.ShapeDtypeStruct((t, _LANES), _F32), jax.ShapeDtypeStruct((n_chunks, _SUBLANES, _LANES), _F32)]


def _router_weight(w_group, w_expert):
    d = w_group.shape[0]
    pad = jnp.zeros((d, _LANES - _NG - _NE), _F32)
    return jnp.concatenate([w_group, w_expert, pad], axis=1)


def _mix_ln_route_kernel(glu, tpc, f_ref, x_ref, w_ref, lng_ref, lnb_ref, wr_ref,
                         o_ref, route_ref, w1b_ref, w2b_ref, cnt_ref, carry_ref):
    i = pl.program_id(0)
    h = _dot(f_ref[...], w_ref[...])
    if glu:
        d = h.shape[1] // 2
        h = h[:, :d] * jax.nn.sigmoid(h[:, d:])
    xn = _layer_norm(_ALPHA * x_ref[...] + h, lng_ref[...], lnb_ref[...])
    o_ref[...] = xn
    _route(xn, wr_ref, i % tpc == 0, carry_ref, route_ref, w1b_ref, w2b_ref, cnt_ref)


def _mix_ln_route(f, f_map, f_shape_block, x, w, ln_g, ln_b, wr, tm, tk, glu):
    t, d = x.shape
    tpc = tk // tm
    n_chunks = t // tk
    kin, n = w.shape
    row = lambda a: a.reshape(1, d)
    const = lambda i: (0, 0)
    return pl.pallas_call(
        functools.partial(_mix_ln_route_kernel, glu, tpc),
        grid=(t // tm,),
        in_specs=[pl.BlockSpec(f_shape_block, f_map), pl.BlockSpec((tm, d), lambda i: (i, 0)),
                  pl.BlockSpec((kin, n), const), pl.BlockSpec((1, d), const), pl.BlockSpec((1, d), const),
                  pl.BlockSpec((d, _LANES), const)],
        out_specs=[pl.BlockSpec((tm, d), lambda i: (i, 0))] + _route_specs(tm, tpc, lambda i: i),
        out_shape=[jax.ShapeDtypeStruct((t, d), _F32)] + _route_shapes(t, n_chunks),
        scratch_shapes=[pltpu.VMEM((1, _LANES), _F32)],
        compiler_params=_params("arbitrary"),
        name="mix_ln_route",
    )(f, x, _mx(w), row(ln_g), row(ln_b), wr)


def _moe_plan(route, cnt, tk):
    t = route.shape[0]
    n_chunks = t // tk
    ri = route.astype(jnp.int32)
    e1, e2, r1, r2 = (ri[:, k].reshape(n_chunks, tk) for k in range(4))
    counts = cnt[:, 0, _NG:_NG + _NE].astype(jnp.int32)
    padded = (counts + _SUBLANES - 1) // _SUBLANES * _SUBLANES
    offs = jnp.cumsum(padded, axis=1) - padded
    eids = jnp.arange(_NE, dtype=jnp.int32)
    lookup = lambda ee: jnp.sum(jnp.where(ee[..., None] == eids, offs[:, None, :], 0), axis=-1)
    dest = ((lookup(e1) + r1) + ((lookup(e2) + r2) << 16)).reshape(t)
    return dest, offs.reshape(-1), counts.reshape(-1)


def _moe_kernel(tk, dest_ref, offs_ref, cnts_ref, x_ref, w1b_ref, w2b_ref, wgu_ref, wd_ref, lng_ref, lnb_ref,
                o_ref, ys_ref, gath_ref, inv_ref):
    c, e = pl.program_id(0), pl.program_id(1)
    n_slots = inv_ref.shape[0]
    d = x_ref.shape[1]

    grp = _SUBLANES

    @pl.when((c == 0) & (e == 0))
    def _():
        gath_ref[...] = jnp.zeros_like(gath_ref)

    @pl.when(e == 0)
    def _():
        def fill(i, _):
            for k in range(grp):
                t = i * grp + k
                dd = dest_ref[c * tk + t]
                inv_ref[dd & 0xFFFF] = t
                inv_ref[dd >> 16] = t
            return 0
        lax.fori_loop(0, tk // grp, fill, 0)

    n = cnts_ref[c * _NE + e]
    off = offs_ref[c * _NE + e]
    pad_end = off + (n + grp - 1) // grp * grp
    for k in range(grp - 1):
        idx = off + n + k
        inv_ref[jnp.where(idx < pad_end, idx, n_slots - 1)] = 0

    def block(j, _):
        base = pl.multiple_of(off + j * _MOE_ROWS, grp)
        n_here = jnp.minimum(n - j * _MOE_ROWS, _MOE_ROWS)

        def gather(g, _):
            for k in range(grp):
                gath_ref[g, k:k + 1, :] = x_ref[pl.ds(inv_ref[base + g * grp + k], 1), :]
            return 0
        lax.fori_loop(0, (n_here + grp - 1) // grp, gather, 0)
        h = _dot(gath_ref[...].reshape(_MOE_ROWS, d), wgu_ref[...])
        act = jax.nn.silu(h[:, :_D_EXPERT]) * h[:, _D_EXPERT:]
        ys_ref[pl.ds(base, _MOE_ROWS), :] = _dot(act, wd_ref[...])
        return 0
    lax.fori_loop(0, (n + _MOE_ROWS - 1) // _MOE_ROWS, block, 0)

    @pl.when(e == _NE - 1)
    def _():
        reps = d // _LANES

        def combine(i, _):
            for k in range(grp):
                t = i * grp + k
                dd = dest_ref[c * tk + t]
                y1 = ys_ref[pl.ds(dd & 0xFFFF, 1), :]
                y2 = ys_ref[pl.ds(dd >> 16, 1), :]
                w1 = jnp.tile(w1b_ref[i, k:k + 1, :], (1, reps))
                w2 = jnp.tile(w2b_ref[i, k:k + 1, :], (1, reps))
                o_ref[i, k:k + 1, :] = _ALPHA * x_ref[pl.ds(t, 1), :] + (y1 * w1 + y2 * w2)
            return 0
        lax.fori_loop(0, tk // grp, combine, 0)
        gt = min(tk, 256) // grp
        for g0 in range(0, tk // grp, gt):
            v = o_ref[g0:g0 + gt].reshape(gt * grp, d)
            o_ref[g0:g0 + gt] = _layer_norm(v, lng_ref[...], lnb_ref[...]).reshape(gt, grp, d)


def _moe(x, route, w1b, w2b, cnt, w_gate_up, w_down, ln_g, ln_b, tk):
    t, d = x.shape
    n_chunks = t // tk
    dest, offs, cnts = _moe_plan(route, cnt, tk)
    n_slots = 2 * tk + _NE * _SUBLANES + _MOE_ROWS
    grp = _SUBLANES
    chunk = lambda c, e, *_: (c, 0)
    chunk3 = lambda c, e, *_: (c, 0, 0)
    const = lambda c, e, *_: (0, 0)
    expert = lambda c, e, *_: (e, 0, 0)
    grid_spec = pltpu.PrefetchScalarGridSpec(
        num_scalar_prefetch=3,
        grid=(n_chunks, _NE),
        in_specs=[pl.BlockSpec((tk, d), chunk, pipeline_mode=pl.Buffered(1)),
                  pl.BlockSpec((tk // grp, grp, _LANES), chunk3, pipeline_mode=pl.Buffered(1)),
                  pl.BlockSpec((tk // grp, grp, _LANES), chunk3, pipeline_mode=pl.Buffered(1)),
                  pl.BlockSpec((None, d, 2 * _D_EXPERT), expert), pl.BlockSpec((None, _D_EXPERT, d), expert),
                  pl.BlockSpec((1, d), const), pl.BlockSpec((1, d), const)],
        out_specs=pl.BlockSpec((tk // grp, grp, d), chunk3),
        scratch_shapes=[pltpu.VMEM((n_slots, d), _F32), pltpu.VMEM((_MOE_ROWS // grp, grp, d), _F32),
                        pltpu.SMEM((n_slots,), jnp.int32)],
    )
    return pl.pallas_call(
        functools.partial(_moe_kernel, tk),
        grid_spec=grid_spec,
        out_shape=jax.ShapeDtypeStruct((t // grp, grp, d), _F32),
        compiler_params=_params("arbitrary", "arbitrary"),
        name="moe",
    )(dest, offs, cnts, x, w1b.reshape(t // grp, grp, _LANES), w2b.reshape(t // grp, grp, _LANES),
      w_gate_up, w_down, ln_g.reshape(1, d), ln_b.reshape(1, d)).reshape(t, d)


def _layer0(x, h0, s5w, w_in, d_skip, w_glu, ln_g, ln_b, moe_w, tm, tk, tc_len):
    b, length, d = x.shape
    nbh = b // _BSUB
    bblk, cblk, a = s5w
    x2 = x.reshape(b * length, d)
    if length % tm == 0:
        nt = length // tm
        u_tm = _proj(x2, w_in, tm, [(0, d)], [_F32], grid=(b, nt), x_map=lambda s, j: (s * nt + j, 0),
                     out_shapes=[(nbh * length, _BSUB * d)],
                     out_maps=[lambda s, j: ((s // _BSUB) * nt + j, s % _BSUB)])[0]
        u_tm = u_tm.reshape(nbh, length * _BSUB, d)
    else:
        u = _proj(x2, w_in, b * length, [(0, d)], [_F32])[0]
        u_tm = jnp.transpose(u.reshape(nbh, _BSUB, length, d), (0, 2, 1, 3)).reshape(nbh, length * _BSUB, d)
    g_tm, h_t = _s5_scan(u_tm, bblk, cblk, a, d_skip, _state_to_blocks(h0), tc_len)
    if length % tm == 0:
        nt = length // tm
        f = g_tm.reshape(nbh * length, _BSUB * d)
        f_map = lambda i: (((i // nt) // _BSUB) * nt + i % nt, (i // nt) % _BSUB)
    else:
        f = jnp.transpose(g_tm.reshape(nbh, length, _BSUB, d), (0, 2, 1, 3)).reshape(b * length, d)
        f_map = lambda i: (i, 0)
    w_group, w_expert, w_gate_up, w_down = moe_w
    wr = _router_weight(w_group, w_expert)
    x1, route, w1b, w2b, cnt = _mix_ln_route(f, f_map, (tm, d), x2, w_glu, ln_g[0], ln_b[0], wr, tm, tk, True)
    x2o = _moe(x1, route, w1b, w2b, cnt, w_gate_up, w_down, ln_g[1], ln_b[1], tk)
    return x2o, _blocks_to_state(h_t)


_HALF = _LANES // 2
_N_PAIR = _N_KV // 2
_KV_HALF = _N_KV * _HEAD_DIM


def _head_perm():
    idx = []
    for p in range(_N_PAIR):
        for h in range(_HPG):
            for gs in range(2):
                base = ((2 * p + gs) * _HPG + h) * _HEAD_DIM
                idx.extend(range(base, base + _HEAD_DIM))
    return np.asarray(idx, np.int32)


def _gate_col(p, h, gs, branch):
    return ((2 * p + gs) * _HPG + h) * _N_BRANCH + branch


def _nsa_weights(kv_w, cmp_pe, cmp_w1, cmp_w2, w_qg, w_o):
    nq = _N_HEADS * _HEAD_DIM
    perm = _head_perm()
    w_q = w_qg[:, :nq][:, perm] * _ATTN_SCALE
    w_g = w_qg[:, nq:]
    w_g = jnp.pad(w_g, ((0, 0), (0, _LANES - w_g.shape[1])))
    eye = jnp.eye(2, dtype=_F32)
    w1r = cmp_w1.reshape(2, 2, _CMP_STRIDE, _HEAD_DIM, _CMP_HIDDEN)
    w1 = jnp.einsum("kzsdh,gG->ksgdzGh", w1r, eye).reshape(2, _CMP_STRIDE * _LANES, 4 * _CMP_HIDDEN)
    w2 = jnp.einsum("khd,gG->kghGd", cmp_w2, eye).reshape(2, 2 * _CMP_HIDDEN, _LANES)
    pe = jnp.pad(cmp_pe.reshape(2, 2, _CMP_STRIDE, _HEAD_DIM), ((0, 0),) * 3 + ((0, _LANES - _HEAD_DIM),))
    pe = pe.reshape(2, 2, 1, _CMP_STRIDE * _LANES)
    return dict(kv_w=kv_w, w_qg=jnp.concatenate([w_q, w_g], axis=1), w_o=w_o[perm], w1=_mx(w1), w2=_mx(w2), pe=pe)


def _sel_sum_matrix(n_rows, n_cmp, n_sel, lanes):
    m = np.zeros((n_rows, lanes), np.float32)
    for c in range(n_cmp):
        j0 = (c * _CMP_STRIDE) // _SEL_BLOCK
        j1 = (c * _CMP_STRIDE + _CMP_BLOCK - 1) // _SEL_BLOCK
        if j0 < n_sel:
            m[c, j0] += 1.0
        if j1 != j0 and j1 < n_sel:
            m[c, j1] += 1.0
    return jnp.asarray(m, _MXU_DTYPE)


def _expand_matrix(lanes, n_keys):
    j = np.arange(lanes)[:, None]
    key = np.arange(n_keys)[None, :]
    return jnp.asarray((key // _SEL_BLOCK == j).astype(np.float32), _MXU_DTYPE)


_CMP_PAGES = 16
_TB_PITCH = 24


def _compress_kernel(npg, transposed, pt_ref, *refs):
    page_refs = refs[:npg + 1]
    w1_ref, w2_ref, pe_ref, o_ref, lhs_ref, tbuf_ref = refs[npg + 1:]
    nlb = 2 * _N_PAIR
    page = page_refs[0].shape[1] if transposed else page_refs[0].shape[0]
    spp = page // _CMP_STRIDE
    seg = npg * spp + _SUBLANES
    rows = _N_PAIR * seg
    n_out = npg * spp
    pe_rows = (n_out + 2, n_out + 3)
    hid2 = 2 * _CMP_HIDDEN

    @pl.when((pl.program_id(0) == 0) & (pl.program_id(1) == 0))
    def _():
        lhs_ref[...] = jnp.zeros_like(lhs_ref)
        for k in range(2):
            for half in range(2):
                lhs_ref[k, pe_rows[half]:pe_rows[half] + 1, :] = pe_ref[k, half]

    def lane_block(ref, c):
        if transposed:
            return jnp.transpose(ref[c * _LANES:(c + 1) * _LANES, :])
        return ref[:, c * _LANES:(c + 1) * _LANES]

    for p in range(npg + 1):
        nsb = spp if p < npg else 1
        for c in range(nlb):
            k, pair = divmod(c, _N_PAIR)
            blk = lane_block(page_refs[p], c)
            for n in range(nsb):
                tbuf_ref[p, c, n * _TB_PITCH:n * _TB_PITCH + _CMP_STRIDE, :] = blk[n * _CMP_STRIDE:(n + 1) * _CMP_STRIDE]
            r0 = pair * seg + p * spp
            for s in range(_CMP_STRIDE):
                lhs_ref[k, r0:r0 + nsb, s * _LANES:(s + 1) * _LANES] = tbuf_ref[p, c, pl.ds(s, nsb, stride=_TB_PITCH), :]

    for k in range(2):
        res = _dot(lhs_ref[k], w1_ref[k])
        lo = res[:, :hid2]
        hi = pltpu.roll(res[:, hid2:], rows - 1, axis=0)
        pe = res[pe_rows[0]:pe_rows[0] + 1, :_CMP_HIDDEN] + res[pe_rows[1]:pe_rows[1] + 1, hid2:hid2 + _CMP_HIDDEN]
        out = _dot(jax.nn.gelu(lo + hi + jnp.tile(pe, (1, 2))), w2_ref[k])
        for pair in range(_N_PAIR):
            c = k * _N_PAIR + pair
            o_ref[:, c * _LANES:(c + 1) * _LANES] = out[pair * seg:pair * seg + n_out].astype(o_ref.dtype)


def _compress(pages, table, nsa, npg, transposed):
    b, npp = table.shape
    page, width = (pages.shape[2], pages.shape[1]) if transposed else pages.shape[1:]
    nlb = width // _LANES
    spp = page // _CMP_STRIDE
    seg = npg * spp + _SUBLANES
    kdim = _CMP_STRIDE * _LANES

    def page_map(i):
        return lambda s, j, pt: (pt[s * npp + jnp.minimum(j * npg + i, npp - 1)], 0, 0)

    const3 = lambda s, j, pt: (0, 0, 0)
    const4 = lambda s, j, pt: (0, 0, 0, 0)
    grid_spec = pltpu.PrefetchScalarGridSpec(
        num_scalar_prefetch=1,
        grid=(b, npp // npg),
        in_specs=[pl.BlockSpec((None,) + pages.shape[1:], page_map(i)) for i in range(npg + 1)]
        + [pl.BlockSpec((2, kdim, 4 * _CMP_HIDDEN), const3, pipeline_mode=pl.Buffered(1)),
           pl.BlockSpec((2, 2 * _CMP_HIDDEN, _LANES), const3), pl.BlockSpec((2, 2, 1, kdim), const4)],
        out_specs=pl.BlockSpec((None, npg * spp, width), lambda s, j, pt: (s, j, 0)),
        scratch_shapes=[pltpu.VMEM((2, _N_PAIR * seg, kdim), _F32),
                        pltpu.VMEM((npg + 1, nlb, spp * _TB_PITCH, _LANES), _F32)],
    )
    return pl.pallas_call(
        functools.partial(_compress_kernel, npg, transposed),
        grid_spec=grid_spec,
        out_shape=jax.ShapeDtypeStruct((b, npp * spp, width), _MXU_DTYPE),
        compiler_params=_params("arbitrary", "arbitrary"),
        name="nsa_compress",
    )(table.reshape(-1), *([pages] * (npg + 1)), nsa["w1"], nsa["w2"], nsa["pe"])


def _softmax_parts(s, mask):
    sm = jnp.where(mask, s, _NEG_INF)
    m = jnp.max(sm, -1, keepdims=True)
    e = jnp.where(mask, jnp.exp(sm - m), 0.0)
    return e, jnp.maximum(jnp.sum(e, -1, keepdims=True), _TINY)


def _dot_exact_rhs(a, b):
    a1 = _mx(a)
    r1 = a - a1.astype(_F32)
    a2 = _mx(r1)
    a3 = _mx(r1 - a2.astype(_F32))
    return _dot(a1, b) + (_dot(a2, b) + _dot(a3, b))


def _top_k_mask(score, k):
    lane = lax.broadcasted_iota(jnp.int32, score.shape, 1).astype(_F32)
    sel = jnp.zeros(score.shape, _F32)
    for _ in range(k):
        m = jnp.max(score, -1, keepdims=True)
        idx = jnp.min(jnp.where(score == m, lane, float(score.shape[1])), -1, keepdims=True)
        hit = lane == idx
        sel = jnp.where(hit, 1.0, sel)
        score = jnp.where(hit, -jnp.inf, score)
    return sel


def _select_mask(pg, mt_ref, q_pos, n_sel, k_sel):
    ps = _dot_exact_rhs(pg, mt_ref[...])
    j = lax.broadcasted_iota(jnp.int32, ps.shape, 1)
    valid = j * _SEL_BLOCK <= q_pos
    own = jnp.right_shift(q_pos, int(math.log2(_SEL_BLOCK)))
    forced = jnp.where(j == own, 1.0, jnp.where(j == 0, 1.0, 0.0))
    score = jnp.where(valid, ps + _FORCE_SCORE * forced, -1.0)
    score = jnp.where(j < n_sel, score, -jnp.inf)
    return _top_k_mask(score, k_sel)


def _masked_halves(blk):
    low = lax.broadcasted_iota(jnp.int32, blk.shape, 1) < _HALF
    return jnp.concatenate([jnp.where(low, blk, 0.0), jnp.where(low, 0.0, blk)], axis=0)


def _merge_halves(o, rows):
    low = lax.broadcasted_iota(jnp.int32, (rows, _LANES), 1) < _HALF
    return jnp.where(low, o[:rows], o[rows:])


def _gate_block(gates, p, h, branch):
    low = lax.broadcasted_iota(jnp.int32, (gates.shape[0], _LANES), 1) < _HALF
    c0, c1 = _gate_col(p, h, 0, branch), _gate_col(p, h, 1, branch)
    return jnp.where(low, gates[:, c0:c0 + 1], gates[:, c1:c1 + 1])


_KEY_CHUNK = 512


def _biased_attend(lq, k, v, bias):
    nk = k.shape[0]
    parts = [(c0, min(c0 + _KEY_CHUNK, nk)) for c0 in range(0, nk, _KEY_CHUNK)]
    s = [_dot_nt(lq, k[c0:c1]) + bias[:, c0:c1] for c0, c1 in parts]
    m = functools.reduce(jnp.maximum, [jnp.max(sc, -1, keepdims=True) for sc in s])
    den, acc = 0.0, 0.0
    for sc, (c0, c1) in zip(s, parts):
        e = jnp.exp(sc - m)
        den = den + jnp.sum(e, -1, keepdims=True)
        acc = acc + _dot(e, v[c0:c1])
    return acc * (1.0 / jnp.maximum(den, _TINY))


def _rank_select(score, n_sel, k_sel):
    st = jnp.transpose(score)
    top = -(-n_sel // _SUBLANES) * _SUBLANES
    s_top = st[:top]
    ridx = lax.broadcasted_iota(jnp.int32, s_top.shape, 0)
    ahead = jnp.zeros(s_top.shape, _F32)
    for j in range(n_sel):
        sj = st[j:j + 1, :]
        ahead = ahead + jnp.where(sj > s_top, 1.0, 0.0) + jnp.where(sj == s_top, jnp.where(ridx > j, 1.0, 0.0), 0.0)
    sel_t = jnp.where(ahead < k_sel, 1.0, 0.0)
    sel_t = jnp.concatenate([sel_t, jnp.zeros((st.shape[0] - top, st.shape[1]), _F32)], axis=0)
    return jnp.transpose(sel_t)


def _attn_prompt_kernel(tq, tile0, n_sel, k_sel, q_ref, gate_ref, ckv_ref, slc_ref, win_ref, mt_ref, ex_ref, o_ref):
    t0 = (tile0 + pl.program_id(1)) * tq
    nk = slc_ref.shape[0]
    nc = ckv_ref.shape[0]
    q_pos = t0 + lax.broadcasted_iota(jnp.int32, (tq, 1), 0)
    qf = q_ref[...].astype(_F32)
    gates = gate_ref[...]
    c_end = lax.broadcasted_iota(jnp.int32, (1, nc), 1) * _CMP_STRIDE + (_CMP_BLOCK - 1)
    cmask = c_end <= q_pos
    cmask2 = jnp.concatenate([cmask, cmask], axis=0)
    causal_bias = jnp.where(lax.broadcasted_iota(jnp.int32, (1, nk), 1) <= q_pos, 0.0, _NEG_INF)
    wk = min(nk, tq + _WINDOW)
    wstart = pl.multiple_of(jnp.clip(t0 - _WINDOW, 0, nk - wk), tq)
    dist = q_pos - (wstart + lax.broadcasted_iota(jnp.int32, (1, wk), 1))
    wbias = jnp.where(dist >= 0, jnp.where(dist < _WINDOW, 0.0, _NEG_INF), _NEG_INF)
    wbias2 = jnp.concatenate([wbias, wbias], axis=0)
    j = lax.broadcasted_iota(jnp.int32, (tq, _LANES), 1)
    valid = j * _SEL_BLOCK <= q_pos
    own = jnp.right_shift(q_pos, int(math.log2(_SEL_BLOCK)))
    forced = jnp.where(j == own, _FORCE_SCORE, jnp.where(j == 0, _FORCE_SCORE, 0.0))

    for p in range(_N_PAIR):
        ksl = slice(p * _LANES, (p + 1) * _LANES)
        vsl = slice(_KV_HALF + p * _LANES, _KV_HALF + (p + 1) * _LANES)
        kc, vc = ckv_ref[:, ksl], ckv_ref[:, vsl]
        lhs, o_cmp = [], []
        pg = [jnp.zeros((tq, nc), _F32), jnp.zeros((tq, nc), _F32)]
        for h in range(_HPG):
            blk = p * _HPG + h
            lq = _mx(_masked_halves(qf[:, blk * _LANES:(blk + 1) * _LANES]))
            lhs.append(lq)
            e, den = _softmax_parts(_dot_nt(lq, kc), cmask2)
            pc = e * (1.0 / den)
            o_cmp.append(_merge_halves(_dot(pc, vc), tq))
            pg = [pg[0] + pc[:tq], pg[1] + pc[tq:]]
        sbias = []
        for gs in range(2):
            ps = _dot_exact_rhs(pg[gs], mt_ref[...])
            score = jnp.where(valid, ps + forced, -1.0)
            score = jnp.where(j < n_sel, score, -jnp.inf)
            sel = _rank_select(score, n_sel, k_sel)
            picked = _dot(sel, ex_ref[...])
            sbias.append(jnp.minimum(causal_bias, (picked - 1.0) * (-_NEG_INF)))
        sbias2 = jnp.concatenate(sbias, axis=0)
        ks, vs = slc_ref[:, ksl], slc_ref[:, vsl]
        kw, vw = win_ref[pl.ds(wstart, wk), ksl], win_ref[pl.ds(wstart, wk), vsl]
        for h in range(_HPG):
            blk = p * _HPG + h
            o_slc = _merge_halves(_biased_attend(lhs[h], ks, vs, sbias2), tq)
            o_win = _merge_halves(_biased_attend(lhs[h], kw, vw, wbias2), tq)
            o = (_gate_block(gates, p, h, 0) * o_cmp[h] + _gate_block(gates, p, h, 1) * o_slc
                 + _gate_block(gates, p, h, 2) * o_win)
            o_ref[:, blk * _LANES:(blk + 1) * _LANES] = o.astype(o_ref.dtype)


def _nsa_prompt(x, b, length, nsa, tm):
    t, d = x.shape
    kvw = 2 * _KV_HALF
    cmp_f, slc_f, win_f, slc_b, win_b = _proj(
        x, nsa["kv_w"], tm, [(0, kvw), (kvw, 2 * kvw), (2 * kvw, 3 * kvw), (kvw, 2 * kvw), (2 * kvw, 3 * kvw)],
        [_F32, _F32, _F32, _MXU_DTYPE, _MXU_DTYPE])
    nq = _N_HEADS * _HEAD_DIM
    q, gates = _proj(x, nsa["w_qg"], tm, [(0, nq), (nq, nq + _LANES)], [_MXU_DTYPE, _F32], acts=[None, "sigmoid"])
    page = _LANES
    npp = length // page
    table = jnp.arange(b * npp, dtype=jnp.int32).reshape(b, npp)
    ckv = _compress(cmp_f.reshape(b * npp, page, kvw), table, nsa, min(_CMP_PAGES, npp), False)
    nc = length // _CMP_STRIDE
    tq = _LANES
    nq_t = length // tq
    mt = _sel_sum_matrix(nc, nc - 1, -(-length // _SEL_BLOCK), _LANES)
    slc3, win3 = slc_b.reshape(b, length, kvw), win_b.reshape(b, length, kvw)
    seq = lambda s, i: (s, 0, 0)
    const = lambda s, i: (0, 0)
    n_buckets = 4 if nq_t % 4 == 0 else 1
    tiles = nq_t // n_buckets
    outs = []
    for bk in range(n_buckets):
        tile0 = bk * tiles
        nk = (tile0 + tiles) * tq
        n_sel = nk // _SEL_BLOCK
        nkc = nk // _CMP_STRIDE
        q_map = lambda s, i, tile0=tile0: (s * nq_t + tile0 + i, 0)
        outs.append(pl.pallas_call(
            functools.partial(_attn_prompt_kernel, tq, tile0, n_sel, min(_N_SELECT, n_sel)),
            grid=(b, tiles),
            in_specs=[pl.BlockSpec((tq, nq), q_map), pl.BlockSpec((tq, _LANES), q_map),
                      pl.BlockSpec((None, nkc, kvw), seq), pl.BlockSpec((None, nk, kvw), seq),
                      pl.BlockSpec((None, nk, kvw), seq),
                      pl.BlockSpec((nkc, _LANES), const), pl.BlockSpec((_LANES, nk), const)],
            out_specs=pl.BlockSpec((None, tq, nq), lambda s, i: (s, i, 0)),
            out_shape=jax.ShapeDtypeStruct((b, tiles * tq, nq), _MXU_DTYPE),
            compiler_params=_params("arbitrary", "arbitrary"),
            name=f"nsa_prompt_attn_{bk}",
        )(q, gates, ckv, slc3, win3, mt[:nkc], _expand_matrix(_LANES, nk)))
    o = jnp.concatenate(outs, axis=1).reshape(t, nq)
    return o, cmp_f, slc_f, win_f


def _attn_sample_cmp_kernel(n_sel, k_sel, past, q_ref, ckv_ref, mt_ref, ocmp_ref, sel_ref):
    s_new = q_ref.shape[0]
    nc = ckv_ref.shape[0]
    q_pos = past + lax.broadcasted_iota(jnp.int32, (s_new, 1), 0)
    qf = q_ref[...].astype(_F32)
    c_end = lax.broadcasted_iota(jnp.int32, (1, nc), 1) * _CMP_STRIDE + (_CMP_BLOCK - 1)
    cmask = c_end <= q_pos
    cmask2 = jnp.concatenate([cmask, cmask], axis=0)
    pg_all = []
    for p in range(_N_PAIR):
        kc = ckv_ref[:, p * _LANES:(p + 1) * _LANES]
        vc = ckv_ref[:, _KV_HALF + p * _LANES:_KV_HALF + (p + 1) * _LANES]
        pg = [jnp.zeros((s_new, nc), _F32), jnp.zeros((s_new, nc), _F32)]
        for h in range(_HPG):
            blk = p * _HPG + h
            lq = _masked_halves(qf[:, blk * _LANES:(blk + 1) * _LANES])
            e, den = _softmax_parts(_dot_nt(lq, kc), cmask2)
            pc = e / den
            ocmp_ref[:, blk * _LANES:(blk + 1) * _LANES] = _merge_halves(_dot(pc, vc), s_new)
            pg = [pg[0] + pc[:s_new], pg[1] + pc[s_new:]]
        pg_all += pg
    sel = _select_mask(jnp.concatenate(pg_all, axis=0), mt_ref, jnp.tile(q_pos, (_N_KV, 1)), n_sel, k_sel)
    for g in range(_N_KV):
        sel_ref[g] = sel[g * s_new:(g + 1) * s_new]


def _attn_sample_kernel(npg, past, pt_ref, *refs):
    page_refs = refs[:npg]
    (q_ref, gate_ref, ocmp_ref, selc_ref, seln_ref, slcn_ref, winc_ref, winn_ref, ex_ref,
     o_ref, m_ref, l_ref, acc_ref) = refs[npg:]
    j = pl.program_id(1)
    s_new = q_ref.shape[0]
    rows = 2 * _HPG * s_new
    qf = q_ref[...].astype(_F32)

    @pl.when(j == 0)
    def _():
        m_ref[...] = jnp.full(m_ref.shape, _NEG_INF, _F32)
        l_ref[...] = jnp.zeros_like(l_ref)
        acc_ref[...] = jnp.zeros_like(acc_ref)

    def pair_lhs(p):
        halves = [_masked_halves(qf[:, (p * _HPG + h) * _LANES:(p * _HPG + h + 1) * _LANES]) for h in range(_HPG)]
        return jnp.concatenate([hv[:s_new] for hv in halves] + [hv[s_new:] for hv in halves], axis=0)

    def online(p, s, mask, pv):
        sm = jnp.where(mask, s, _NEG_INF)
        m_old = m_ref[p]
        m_new = jnp.maximum(m_old, jnp.max(sm, -1, keepdims=True))
        alpha = jnp.exp(m_old - m_new)
        e = jnp.where(mask, jnp.exp(sm - m_new), 0.0)
        l_ref[p] = alpha * l_ref[p] + jnp.sum(e, -1, keepdims=True)
        acc_ref[p] = alpha * acc_ref[p] + pv(e)
        m_ref[p] = m_new

    for p in range(_N_PAIR):
        ksl = slice(p * _LANES, (p + 1) * _LANES)
        vsl = slice(_KV_HALF + p * _LANES, _KV_HALF + (p + 1) * _LANES)
        lq = pair_lhs(p)
        k_t = jnp.concatenate([r[ksl, :] for r in page_refs], axis=1)
        v_t = jnp.concatenate([r[vsl, :] for r in page_refs], axis=1)
        mask = jnp.concatenate(
            [jnp.tile(_dot(selc_ref[2 * p + gs], ex_ref[...]) > 0.5, (_HPG, 1)) for gs in range(2)], axis=0)
        online(p, _dot(lq, k_t), mask, lambda e, v_t=v_t: _dot_nt(e, v_t))

    @pl.when(j == pl.num_programs(1) - 1)
    def _():
        gates = gate_ref[...]
        tok = lax.broadcasted_iota(jnp.int32, (s_new, 1), 0)
        q_pos = past + jnp.tile(tok, (2 * _HPG, 1))
        newer = lax.broadcasted_iota(jnp.int32, (1, s_new), 1) <= jnp.tile(tok, (2 * _HPG, 1))
        n_cache = winc_ref.shape[1]
        k_pos = past - n_cache + lax.broadcasted_iota(jnp.int32, (1, n_cache + s_new), 1)
        dist = q_pos - k_pos
        wmask = (dist >= 0) & (dist < _WINDOW) & (k_pos >= 0)
        wmask_c, wmask_n = wmask[:, :n_cache], wmask[:, n_cache:]
        for p in range(_N_PAIR):
            ksl = slice(p * _LANES, (p + 1) * _LANES)
            vsl = slice(_KV_HALF + p * _LANES, _KV_HALF + (p + 1) * _LANES)
            lq = pair_lhs(p)
            nmask = jnp.concatenate(
                [jnp.tile(seln_ref[2 * p + gs][:, :s_new] > 0.5, (_HPG, 1)) for gs in range(2)], axis=0) & newer
            online(p, _dot_nt(lq, slcn_ref[:, ksl]), nmask, lambda e, vsl=vsl: _dot(e, slcn_ref[:, vsl]))
            o_slc = acc_ref[p] / jnp.maximum(l_ref[p], _TINY)
            s_c = jnp.where(wmask_c, _dot(lq, winc_ref[ksl, :]), _NEG_INF)
            s_n = jnp.where(wmask_n, _dot_nt(lq, winn_ref[:, ksl]), _NEG_INF)
            m = jnp.maximum(jnp.max(s_c, -1, keepdims=True), jnp.max(s_n, -1, keepdims=True))
            e_c = jnp.where(wmask_c, jnp.exp(s_c - m), 0.0)
            e_n = jnp.where(wmask_n, jnp.exp(s_n - m), 0.0)
            den = jnp.maximum(jnp.sum(e_c, -1, keepdims=True) + jnp.sum(e_n, -1, keepdims=True), _TINY)
            o_win = (_dot_nt(e_c, winc_ref[vsl, :]) + _dot(e_n, winn_ref[:, vsl])) / den
            for h in range(_HPG):
                blk = p * _HPG + h
                lo, hi = h * s_new, (_HPG + h) * s_new
                pick = lambda o: _merge_halves(jnp.concatenate([o[lo:lo + s_new], o[hi:hi + s_new]], axis=0), s_new)
                o = (_gate_block(gates, p, h, 0) * ocmp_ref[:, blk * _LANES:(blk + 1) * _LANES]
                     + _gate_block(gates, p, h, 1) * pick(o_slc) + _gate_block(gates, p, h, 2) * pick(o_win))
                o_ref[:, blk * _LANES:(blk + 1) * _LANES] = o


def _nsa_sample(x, b, s_new, nsa, cache_cmp_kv, cache_slc_kv, cache_win_kv, page_table):
    t, d = x.shape
    kvw = 2 * _KV_HALF
    nq = _N_HEADS * _HEAD_DIM
    n_pool, page = cache_cmp_kv.shape[:2]
    npp = page_table.shape[1]
    past = npp * page
    assert (past + s_new) // _CMP_STRIDE == past // _CMP_STRIDE and past % _SEL_BLOCK == 0
    cmp_f, slc_f, win_f = _proj(x, nsa["kv_w"], t, [(0, kvw), (kvw, 2 * kvw), (2 * kvw, 3 * kvw)], [_F32] * 3)
    q, gates = _proj(x, nsa["w_qg"], t, [(0, nq), (nq, nq + _LANES)], [_F32, _F32], acts=[None, "sigmoid"])
    kv_t = lambda c: jnp.transpose(c, (0, 2, 3, 4, 1)).reshape(c.shape[0], kvw, c.shape[1])
    ckv = _compress(kv_t(cache_cmp_kv), page_table, nsa, min(_CMP_PAGES, npp), True)
    nc = past // _CMP_STRIDE
    n_sel = -(-(past + s_new) // _SEL_BLOCK)
    sel_lanes = -(-n_sel // _LANES) * _LANES
    mt = _sel_sum_matrix(nc, nc - 1, n_sel, sel_lanes)
    seq2 = lambda s: (s, 0)
    o_cmp, sel = pl.pallas_call(
        functools.partial(_attn_sample_cmp_kernel, n_sel, min(_N_SELECT, n_sel), past),
        grid=(b,),
        in_specs=[pl.BlockSpec((s_new, nq), seq2), pl.BlockSpec((None, nc, kvw), lambda s: (s, 0, 0)),
                  pl.BlockSpec((nc, sel_lanes), lambda s: (0, 0))],
        out_specs=[pl.BlockSpec((s_new, nq), seq2),
                   pl.BlockSpec((None, _N_KV, s_new, sel_lanes), lambda s: (s, 0, 0, 0))],
        out_shape=[jax.ShapeDtypeStruct((t, nq), _F32), jax.ShapeDtypeStruct((b, _N_KV, s_new, sel_lanes), _F32)],
        compiler_params=_params("arbitrary"),
    )(q, ckv, mt)

    npg = min(16, npp)
    n_chunks = npp // npg
    bpc = npg * page // _SEL_BLOCK
    n_past_blocks = past // _SEL_BLOCK
    selc = sel[..., :n_past_blocks].reshape(b, _N_KV, s_new, n_chunks, bpc)
    selc = jnp.pad(jnp.transpose(selc, (0, 3, 1, 2, 4)), ((0, 0),) * 4 + ((0, _LANES - bpc),))
    seln = jnp.broadcast_to(sel[..., n_past_blocks:n_past_blocks + 1], (b, _N_KV, s_new, _LANES))
    ex = _expand_matrix(_LANES, npg * page)
    n_cache = cache_win_kv.shape[1]

    def page_map(i):
        return lambda s, j, pt: (pt[s * npp + j * npg + i], 0, 0)

    seq = lambda s, j, pt: (s, 0)
    seq3 = lambda s, j, pt: (s, 0, 0)
    grid_spec = pltpu.PrefetchScalarGridSpec(
        num_scalar_prefetch=1,
        grid=(b, n_chunks),
        in_specs=[pl.BlockSpec((None, kvw, page), page_map(i)) for i in range(npg)]
        + [pl.BlockSpec((s_new, nq), seq), pl.BlockSpec((s_new, _LANES), seq), pl.BlockSpec((s_new, nq), seq),
           pl.BlockSpec((None, None, _N_KV, s_new, _LANES), lambda s, j, pt: (s, j, 0, 0, 0)),
           pl.BlockSpec((None, _N_KV, s_new, _LANES), lambda s, j, pt: (s, 0, 0, 0)),
           pl.BlockSpec((s_new, kvw), seq), pl.BlockSpec((None, kvw, n_cache), seq3), pl.BlockSpec((s_new, kvw), seq),
           pl.BlockSpec((_LANES, npg * page), lambda s, j, pt: (0, 0))],
        out_specs=pl.BlockSpec((s_new, nq), seq),
        scratch_shapes=[pltpu.VMEM((_N_PAIR, 2 * _HPG * s_new, 1), _F32), pltpu.VMEM((_N_PAIR, 2 * _HPG * s_new, 1), _F32),
                        pltpu.VMEM((_N_PAIR, 2 * _HPG * s_new, _LANES), _F32)],
    )
    o = pl.pallas_call(
        functools.partial(_attn_sample_kernel, npg, past),
        grid_spec=grid_spec,
        out_shape=jax.ShapeDtypeStruct((t, nq), _F32),
        compiler_params=_params("arbitrary", "arbitrary"),
        name="nsa_sample_attn",
    )(page_table.reshape(-1), *([kv_t(cache_slc_kv)] * npg), q, gates, o_cmp, selc, seln,
      slc_f, kv_t(cache_win_kv), win_f, ex)
    return o, cmp_f, slc_f, win_f


def kernel(x_prompt, x_sample, state_ssm, cache_cmp_kv, cache_slc_kv, cache_win_kv, page_table, a_w_in, a_lambda_re, a_lambda_im, a_log_dt, a_b_re, a_b_im, a_c_re, a_c_im, a_d, a_w_glu, kv_w, cmp_pe, cmp_w1, cmp_w2, b_w_qg, b_w_o, moe_w_group, moe_w_expert, moe_w_gate_up, moe_w_down, ln_g, ln_b):
    d = x_prompt.shape[-1]
    bp, lp, _ = x_prompt.shape
    bs, ls, _ = x_sample.shape
    tm_p, tk_p, tc_p = min(512, lp), min(2048, bp * lp), min(64, lp)
    ts = bs * ls

    ar, ai, bbr, bbi = _s5_discretize(a_lambda_re[0], a_lambda_im[0], a_log_dt[0], a_b_re[0], a_b_im[0])
    s5w = _s5_block_weights(ar, ai, bbr, bbi, a_c_re[0], a_c_im[0])
    moe0 = (moe_w_group[0], moe_w_expert[0], moe_w_gate_up[0], moe_w_down[0])
    h0p = jnp.zeros((bp,) + state_ssm.shape[2:], _F32)
    xp, hp = _layer0(x_prompt, h0p, s5w, a_w_in[0], a_d[0], a_w_glu[0], ln_g[0], ln_b[0], moe0, tm_p, tk_p, tc_p)
    xs, hs = _layer0(x_sample, state_ssm[0], s5w, a_w_in[0], a_d[0], a_w_glu[0], ln_g[0], ln_b[0], moe0, ts, ts, ls)

    nsa = _nsa_weights(kv_w, cmp_pe, cmp_w1, cmp_w2, b_w_qg[0], b_w_o[0])
    moe1 = (moe_w_group[1], moe_w_expert[1], moe_w_gate_up[1], moe_w_down[1])
    wr1 = _router_weight(moe1[0], moe1[1])
    kvw = 2 * _N_KV * _HEAD_DIM
    kv_shape = (2, _N_KV, _HEAD_DIM)

    op, cmp_p, slc_p, win_p = _nsa_prompt(xp, bp, lp, nsa, tm_p)
    x1p, route, w1b, w2b, cnt = _mix_ln_route(op, lambda i: (i, 0), (tm_p, d), xp, nsa["w_o"], ln_g[1, 0], ln_b[1, 0],
                                             wr1, tm_p, tk_p, False)
    yp = _moe(x1p, route, w1b, w2b, cnt, moe1[2], moe1[3], ln_g[1, 1], ln_b[1, 1], tk_p)

    os_, cmp_s, slc_s, win_s = _nsa_sample(xs, bs, ls, nsa, cache_cmp_kv, cache_slc_kv, cache_win_kv, page_table)
    x1s, route, w1b, w2b, cnt = _mix_ln_route(os_, lambda i: (i, 0), (ts, d), xs, nsa["w_o"], ln_g[1, 0], ln_b[1, 0],
                                             wr1, ts, ts, False)
    ys = _moe(x1s, route, w1b, w2b, cnt, moe1[2], moe1[3], ln_g[1, 1], ln_b[1, 1], ts)

    n_keep = min(_WINDOW, lp)
    p_win = win_p.reshape((bp, lp) + kv_shape)[:, lp - n_keep:]
    win_all = jnp.concatenate([cache_win_kv, win_s.reshape((bs, ls) + kv_shape).astype(cache_win_kv.dtype)], axis=1)
    s_win = win_all[:, win_all.shape[1] - min(_WINDOW, win_all.shape[1]):]
    return (yp.reshape(bp, lp, d), ys.reshape(bs, ls, d), hp[None],
            cmp_p.reshape((bp, lp) + kv_shape), slc_p.reshape((bp, lp) + kv_shape), p_win,
            hs[None].astype(state_ssm.dtype),
            cmp_s.reshape((bs, ls) + kv_shape), slc_s.reshape((bs, ls) + kv_shape), s_win)
```

```python
import functools
import math

import jax
import jax.numpy as jnp
import numpy as np
from jax import lax
from jax.experimental import pallas as pl
from jax.experimental.pallas import tpu as pltpu

_F32 = jnp.float32
_MXU_DTYPE = jnp.bfloat16
_LANES = 128
_SUBLANES = 8
_VMEM_LIMIT = 56 * 1024 * 1024

_DEPTH = 2
_SSM_W = 16
_SSM_P = 64
_S5_LANES = _LANES
_GB = _S5_LANES // _SSM_W
_BSUB = 16
_N_HEADS = 16
_HEAD_DIM = 64
_N_KV = 4
_HPG = _N_HEADS // _N_KV
_N_BRANCH = 3
_CMP_BLOCK = 32
_CMP_STRIDE = 16
_CMP_HIDDEN = 256
_SEL_BLOCK = 64
_N_SELECT = 16
_WINDOW = 512
_ATTN_SCALE = _HEAD_DIM ** -0.5
_FORCE_SCORE = 1e4
_NEG_INF = -1e30
_TINY = 1e-30
_NG = 4
_EPG = 8
_NE = _NG * _EPG
_D_EXPERT = 256
_MOE_ROWS = 160
_MIX_ROWS = 256
_ALPHA = (2 * _DEPTH) ** 0.25
_LN_EPS = 1e-5


def _mx(x):
    return x.astype(_MXU_DTYPE)


def _dot(a, b):
    return jnp.dot(_mx(a), _mx(b), preferred_element_type=_F32)


def _dot_nt(a, b):
    return lax.dot_general(_mx(a), _mx(b), (((1,), (1,)), ((), ())), preferred_element_type=_F32)


def _split2(a):
    a1 = _mx(a)
    return a1, _mx(a - a1.astype(_F32))


def _dot_hi(a, b):
    a1, a2 = _split2(a)
    b1, b2 = _split2(b)
    return _dot(a1, b1) + (_dot(a1, b2) + _dot(a2, b1))


def _params(*sem):
    return pltpu.CompilerParams(dimension_semantics=sem, vmem_limit_bytes=_VMEM_LIMIT)


def _layer_norm(v, g, b):
    mu = jnp.mean(v, -1, keepdims=True)
    d = v - mu
    var = jnp.mean(d * d, -1, keepdims=True)
    return d * lax.rsqrt(var + _LN_EPS) * g + b


def _proj_kernel(cols, acts, x_ref, w_ref, *o_refs):
    y = _dot(x_ref[...], w_ref[...])
    for (c0, c1), act, o_ref in zip(cols, acts, o_refs):
        v = y[:, c0:c1]
        if act == "sigmoid":
            v = jax.nn.sigmoid(v)
        o_ref[...] = v.astype(o_ref.dtype)


def _proj(x, w, tm, cols, dtypes, acts=None, grid=None, x_map=None, out_shapes=None, out_maps=None):
    m, k = x.shape
    n = w.shape[1]
    acts = acts or [None] * len(cols)
    if grid is None:
        grid = (m // tm,)
        x_map = lambda i: (i, 0)
        out_shapes = [(m, c1 - c0) for c0, c1 in cols]
        out_maps = [x_map] * len(cols)
    w_map = (lambda *a: (0, 0))
    return pl.pallas_call(
        functools.partial(_proj_kernel, tuple(cols), tuple(acts)),
        grid=grid,
        in_specs=[pl.BlockSpec((tm, k), x_map), pl.BlockSpec((k, n), w_map)],
        out_specs=[pl.BlockSpec((tm, c1 - c0), om) for (c0, c1), om in zip(cols, out_maps)],
        out_shape=[jax.ShapeDtypeStruct(s, d) for s, d in zip(out_shapes, dtypes)],
        compiler_params=_params(*(["arbitrary"] * len(grid))),
        name="proj",
    )(x, _mx(w))


def _s5_disc_kernel(lr_ref, li_ref, ldt_ref, bre_ref, bim_ref, ar_ref, ai_ref, bbr_ref, bbi_ref):
    lr, li = lr_ref[...], li_ref[...]
    dt = jnp.exp(ldt_ref[...])
    mag = jnp.exp(lr * dt)
    a_re, a_im = mag * jnp.cos(li * dt), mag * jnp.sin(li * dt)
    den = lr * lr + li * li
    nr = a_re - 1.0
    z_re = (nr * lr + a_im * li) / den
    z_im = (a_im * lr - nr * li) / den
    b_re, b_im = bre_ref[...], bim_ref[...]
    ar_ref[...] = a_re
    ai_ref[...] = a_im
    bbr_ref[...] = z_re * b_re - z_im * b_im
    bbi_ref[...] = z_re * b_im + z_im * b_re


def _s5_discretize(lam_re, lam_im, log_dt, b_re, b_im):
    g, p = lam_re.shape
    w = b_re.shape[-1]
    n = g * p
    flat = lambda a: a.reshape(1, n)
    b_t = lambda a: a.reshape(n, w).T
    ldt = jnp.repeat(log_dt, p).reshape(1, n)
    outs = pl.pallas_call(
        _s5_disc_kernel,
        out_shape=[jax.ShapeDtypeStruct((1, n), _F32)] * 2 + [jax.ShapeDtypeStruct((w, n), _F32)] * 2,
    )(flat(lam_re), flat(lam_im), ldt, b_t(b_re), b_t(b_im))
    return outs


def _s5_block_weights(abar_re, abar_im, bbt_re, bbt_im, c_re, c_im):
    g = abar_re.shape[1] // _SSM_P
    nb = g // _GB
    eye = jnp.eye(_GB, dtype=_F32)

    def in_mat(bt):
        b4 = bt.reshape(_SSM_W, nb, _GB, _SSM_P)
        return jnp.einsum("wbgp,gh->bgwhp", b4, eye).reshape(nb, _GB * _SSM_W, _GB * _SSM_P)

    def out_mat(c):
        c4 = c.reshape(nb, _GB, _SSM_W, _SSM_P)
        return jnp.einsum("bgwp,gh->bgphw", c4, eye).reshape(nb, _GB * _SSM_P, _GB * _SSM_W)

    bblk = jnp.concatenate([in_mat(bbt_re), in_mat(bbt_im)], axis=2)
    cblk = jnp.concatenate([out_mat(c_re), -out_mat(c_im)], axis=1)
    a = jnp.concatenate([abar_re.reshape(nb, 1, _GB * _SSM_P), abar_im.reshape(nb, 1, _GB * _SSM_P)], axis=2)
    a = jnp.broadcast_to(a, (nb, _BSUB, 2 * _GB * _SSM_P))
    return _mx(bblk), _mx(cblk), a


def _state_to_blocks(h):
    b, g, p, _ = h.shape
    h6 = h.reshape(b // _BSUB, _BSUB, g // _GB, _GB, p, 2)
    return jnp.transpose(h6, (0, 2, 1, 5, 3, 4)).reshape(b // _BSUB, g // _GB, _BSUB, 2 * _GB * p)


def _blocks_to_state(hb):
    nbh, nb, _, _ = hb.shape
    h6 = hb.reshape(nbh, nb, _BSUB, 2, _GB, _SSM_P)
    return jnp.transpose(h6, (0, 2, 1, 4, 5, 3)).reshape(nbh * _BSUB, nb * _GB, _SSM_P, 2)


def _s5_scan_kernel(tc_len, u_ref, bblk_ref, cblk_ref, a_ref, d_ref, h0_ref, g_ref, ht_ref, hbuf, st):
    tci = pl.program_id(2)
    half = _GB * _SSM_P

    @pl.when(tci == 0)
    def _():
        st[...] = h0_ref[...]

    u = u_ref[...]
    hbuf[...] = _dot(u, bblk_ref[...])
    a_re, a_im = a_ref[:, :half], a_ref[:, half:]

    def step(t, carry):
        h_re, h_im = carry
        r0 = pl.multiple_of(t * _BSUB, _BSUB)
        n_re = a_re * h_re - a_im * h_im + hbuf[pl.ds(r0, _BSUB), :half]
        n_im = a_re * h_im + a_im * h_re + hbuf[pl.ds(r0, _BSUB), half:]
        hbuf[pl.ds(r0, _BSUB), :half] = n_re
        hbuf[pl.ds(r0, _BSUB), half:] = n_im
        return n_re, n_im

    h_re, h_im = lax.fori_loop(0, tc_len, step, (st[:, :half], st[:, half:]))
    st[:, :half] = h_re
    st[:, half:] = h_im
    y = _dot(hbuf[...], cblk_ref[...]) + d_ref[...] * u
    g_ref[...] = jax.nn.gelu(y).astype(g_ref.dtype)

    @pl.when(tci == pl.num_programs(2) - 1)
    def _():
        ht_ref[...] = st[...]


def _s5_scan(u_tm, bblk, cblk, a, d_skip, h0_blocks, tc_len):
    nbh, rows, d = u_tm.shape
    nb = d // _S5_LANES
    length = rows // _BSUB
    ntc = length // tc_len
    tr = tc_len * _BSUB
    nstate = 2 * _GB * _SSM_P
    return pl.pallas_call(
        functools.partial(_s5_scan_kernel, tc_len),
        grid=(nbh, nb, ntc),
        in_specs=[
            pl.BlockSpec((None, tr, _S5_LANES), lambda h, b, t: (h, t, b)),
            pl.BlockSpec((None, _S5_LANES, nstate), lambda h, b, t: (b, 0, 0)),
            pl.BlockSpec((None, nstate, _S5_LANES), lambda h, b, t: (b, 0, 0)),
            pl.BlockSpec((None, _BSUB, nstate), lambda h, b, t: (b, 0, 0)),
            pl.BlockSpec((1, _S5_LANES), lambda h, b, t: (0, b)),
            pl.BlockSpec((None, None, _BSUB, nstate), lambda h, b, t: (h, b, 0, 0)),
        ],
        out_specs=[
            pl.BlockSpec((None, tr, _S5_LANES), lambda h, b, t: (h, t, b)),
            pl.BlockSpec((None, None, _BSUB, nstate), lambda h, b, t: (h, b, 0, 0)),
        ],
        out_shape=[jax.ShapeDtypeStruct((nbh, rows, d), _MXU_DTYPE),
                   jax.ShapeDtypeStruct((nbh, nb, _BSUB, nstate), _F32)],
        scratch_shapes=[pltpu.VMEM((tr, nstate), _F32), pltpu.VMEM((_BSUB, nstate), _F32)],
        compiler_params=_params("arbitrary", "arbitrary", "arbitrary"),
        name="s5_scan",
    )(u_tm, bblk, cblk, a, d_skip.reshape(1, d), h0_blocks)


def _route(xn, wr_ref, first_tile, carry_ref, route_ref, w1b_ref, w2b_ref, cnt_ref):
    tm = xn.shape[0]
    logits = _dot_hi(xn, wr_ref[...])
    lane = lax.broadcasted_iota(jnp.int32, logits.shape, 1).astype(_F32)
    neg, big = -jnp.inf, float(_LANES)
    lg = jnp.where(lane < _NG, logits, neg)
    g_top = jnp.max(lg, -1, keepdims=True)
    g_sel = jnp.min(jnp.where(lg == g_top, lane, big), -1, keepdims=True)
    lse = jnp.log(jnp.sum(jnp.exp(lg - g_top), -1, keepdims=True)) + g_top
    g_w = jnp.exp(g_top - lse)
    lo = _NG + g_sel * _EPG
    le = jnp.where((lane >= lo) & (lane < lo + _EPG), logits, neg)
    m1 = jnp.max(le, -1, keepdims=True)
    i1 = jnp.min(jnp.where(le == m1, lane, big), -1, keepdims=True)
    le2 = jnp.where(lane == i1, neg, le)
    m2 = jnp.max(le2, -1, keepdims=True)
    i2 = jnp.min(jnp.where(le2 == m2, lane, big), -1, keepdims=True)
    ex = jnp.exp(m2 - m1)
    w1 = g_w * (1.0 / (1.0 + ex))
    w2 = g_w * (ex / (1.0 + ex))

    @pl.when(first_tile)
    def _():
        carry_ref[...] = jnp.zeros_like(carry_ref)

    hit1, hit2 = lane == i1, lane == i2
    onehot = jnp.where(hit1 | hit2, 1.0, 0.0)
    r = lax.broadcasted_iota(jnp.int32, (tm, tm), 0)
    c = lax.broadcasted_iota(jnp.int32, (tm, tm), 1)
    tri = jnp.where(c < r, 1.0, 0.0)
    pref = _dot(tri, onehot) + carry_ref[...]
    rank1 = jnp.sum(jnp.where(hit1, pref, 0.0), -1, keepdims=True)
    rank2 = jnp.sum(jnp.where(hit2, pref, 0.0), -1, keepdims=True)
    carry = carry_ref[...] + jnp.sum(onehot, 0, keepdims=True)
    carry_ref[...] = carry
    cnt_ref[...] = jnp.broadcast_to(carry, cnt_ref.shape)
    slab = jnp.where(lane == 0, i1 - _NG, jnp.where(lane == 1, i2 - _NG,
                     jnp.where(lane == 2, rank1, jnp.where(lane == 3, rank2, 0.0))))
    route_ref[...] = slab[:, :_SUBLANES]
    w1b_ref[...] = jnp.broadcast_to(w1, w1b_ref.shape)
    w2b_ref[...] = jnp.broadcast_to(w2, w2b_ref.shape)


def _route_specs(tm, tpc, tile_of):
    def tile_map(*ids):
        return (tile_of(*ids), 0)

    def chunk_map(*ids):
        return (tile_of(*ids) // tpc, 0, 0)

    specs = [pl.BlockSpec((tm, _SUBLANES), tile_map), pl.BlockSpec((tm, _LANES), tile_map),
             pl.BlockSpec((tm, _LANES), tile_map), pl.BlockSpec((None, _SUBLANES, _LANES), chunk_map)]
    return specs


def _route_shapes(t, n_chunks):
    return [jax.ShapeDtypeStruct((t, _SUBLANES), _F32), jax.ShapeDtypeStruct((t, _LANES), _F32),
            jax.ShapeDtypeStruct((t, _LANES), _F32), jax.ShapeDtypeStruct((n_chunks, _SUBLANES, _LANES), _F32)]


def _router_weight(w_group, w_expert):
    d = w_group.shape[0]
    pad = jnp.zeros((d, _LANES - _NG - _NE), _F32)
    return jnp.concatenate([w_group, w_expert, pad], axis=1)


def _mix_ln_route_kernel(glu, tpc, f_ref, x_ref, w_ref, lng_ref, lnb_ref, wr_ref,
                         o_ref, route_ref, w1b_ref, w2b_ref, cnt_ref, carry_ref):
    i = pl.program_id(0)
    h = _dot(f_ref[...], w_ref[...])
    if glu:
        d = h.shape[1] // 2
        h = h[:, :d] * jax.nn.sigmoid(h[:, d:])
    xn = _layer_norm(_ALPHA * x_ref[...] + h, lng_ref[...], lnb_ref[...])
    o_ref[...] = xn
    _route(xn, wr_ref, i % tpc == 0, carry_ref, route_ref, w1b_ref, w2b_ref, cnt_ref)


def _mix_ln_route(f, f_map, f_shape_block, x, w, ln_g, ln_b, wr, tm, tk, glu):
    t, d = x.shape
    tpc = tk // tm
    n_chunks = t // tk
    kin, n = w.shape
    row = lambda a: a.reshape(1, d)
    const = lambda i: (0, 0)
    return pl.pallas_call(
        functools.partial(_mix_ln_route_kernel, glu, tpc),
        grid=(t // tm,),
        in_specs=[pl.BlockSpec(f_shape_block, f_map), pl.BlockSpec((tm, d), lambda i: (i, 0)),
                  pl.BlockSpec((kin, n), const), pl.BlockSpec((1, d), const), pl.BlockSpec((1, d), const),
                  pl.BlockSpec((d, _LANES), const)],
        out_specs=[pl.BlockSpec((tm, d), lambda i: (i, 0))] + _route_specs(tm, tpc, lambda i: i),
        out_shape=[jax.ShapeDtypeStruct((t, d), _F32)] + _route_shapes(t, n_chunks),
        scratch_shapes=[pltpu.VMEM((1, _LANES), _F32)],
        compiler_params=_params("arbitrary"),
        name="mix_ln_route",
    )(f, x, _mx(w), row(ln_g), row(ln_b), wr)


def _moe_plan(route, cnt, tk):
    t = route.shape[0]
    n_chunks = t // tk
    ri = route.astype(jnp.int32)
    e1, e2, r1, r2 = (ri[:, k].reshape(n_chunks, tk) for k in range(4))
    counts = cnt[:, 0, _NG:_NG + _NE].astype(jnp.int32)
    padded = (counts + _SUBLANES - 1) // _SUBLANES * _SUBLANES
    offs = jnp.cumsum(padded, axis=1) - padded
    eids = jnp.arange(_NE, dtype=jnp.int32)
    lookup = lambda ee: jnp.sum(jnp.where(ee[..., None] == eids, offs[:, None, :], 0), axis=-1)
    dest = ((lookup(e1) + r1) + ((lookup(e2) + r2) << 16)).reshape(t)
    return dest, offs.reshape(-1), counts.reshape(-1)


def _moe_kernel(tk, dest_ref, offs_ref, cnts_ref, x_ref, w1b_ref, w2b_ref, wgu_ref, wd_ref, lng_ref, lnb_ref,
                o_ref, ys_ref, gath_ref, inv_ref):
    c, e = pl.program_id(0), pl.program_id(1)
    n_slots = inv_ref.shape[0]
    d = x_ref.shape[1]

    grp = _SUBLANES

    @pl.when((c == 0) & (e == 0))
    def _():
        gath_ref[...] = jnp.zeros_like(gath_ref)

    @pl.when(e == 0)
    def _():
        def fill(i, _):
            for k in range(grp):
                t = i * grp + k
                dd = dest_ref[c * tk + t]
                inv_ref[dd & 0xFFFF] = t
                inv_ref[dd >> 16] = t
            return 0
        lax.fori_loop(0, tk // grp, fill, 0)

    n = cnts_ref[c * _NE + e]
    off = offs_ref[c * _NE + e]
    pad_end = off + (n + grp - 1) // grp * grp
    for k in range(grp - 1):
        idx = off + n + k
        inv_ref[jnp.where(idx < pad_end, idx, n_slots - 1)] = 0

    def block(j, _):
        base = pl.multiple_of(off + j * _MOE_ROWS, grp)
        n_here = jnp.minimum(n - j * _MOE_ROWS, _MOE_ROWS)

        def gather(g, _):
            for k in range(grp):
                gath_ref[g, k:k + 1, :] = x_ref[pl.ds(inv_ref[base + g * grp + k], 1), :]
            return 0
        lax.fori_loop(0, (n_here + grp - 1) // grp, gather, 0)
        h = _dot(gath_ref[...].reshape(_MOE_ROWS, d), wgu_ref[...])
        act = jax.nn.silu(h[:, :_D_EXPERT]) * h[:, _D_EXPERT:]
        ys_ref[pl.ds(base, _MOE_ROWS), :] = _dot(act, wd_ref[...])
        return 0
    lax.fori_loop(0, (n + _MOE_ROWS - 1) // _MOE_ROWS, block, 0)

    @pl.when(e == _NE - 1)
    def _():
        reps = d // _LANES

        def combine(i, _):
            for k in range(grp):
                t = i * grp + k
                dd = dest_ref[c * tk + t]
                y1 = ys_ref[pl.ds(dd & 0xFFFF, 1), :]
                y2 = ys_ref[pl.ds(dd >> 16, 1), :]
                w1 = jnp.tile(w1b_ref[i, k:k + 1, :], (1, reps))
                w2 = jnp.tile(w2b_ref[i, k:k + 1, :], (1, reps))
                o_ref[i, k:k + 1, :] = _ALPHA * x_ref[pl.ds(t, 1), :] + (y1 * w1 + y2 * w2)
            return 0
        lax.fori_loop(0, tk // grp, combine, 0)
        gt = min(tk, 256) // grp
        for g0 in range(0, tk // grp, gt):
            v = o_ref[g0:g0 + gt].reshape(gt * grp, d)
            o_ref[g0:g0 + gt] = _layer_norm(v, lng_ref[...], lnb_ref[...]).reshape(gt, grp, d)


def _moe(x, route, w1b, w2b, cnt, w_gate_up, w_down, ln_g, ln_b, tk):
    t, d = x.shape
    n_chunks = t // tk
    dest, offs, cnts = _moe_plan(route, cnt, tk)
    n_slots = 2 * tk + _NE * _SUBLANES + _MOE_ROWS
    grp = _SUBLANES
    chunk = lambda c, e, *_: (c, 0)
    chunk3 = lambda c, e, *_: (c, 0, 0)
    const = lambda c, e, *_: (0, 0)
    expert = lambda c, e, *_: (e, 0, 0)
    grid_spec = pltpu.PrefetchScalarGridSpec(
        num_scalar_prefetch=3,
        grid=(n_chunks, _NE),
        in_specs=[pl.BlockSpec((tk, d), chunk, pipeline_mode=pl.Buffered(1)),
                  pl.BlockSpec((tk // grp, grp, _LANES), chunk3, pipeline_mode=pl.Buffered(1)),
                  pl.BlockSpec((tk // grp, grp, _LANES), chunk3, pipeline_mode=pl.Buffered(1)),
                  pl.BlockSpec((None, d, 2 * _D_EXPERT), expert), pl.BlockSpec((None, _D_EXPERT, d), expert),
                  pl.BlockSpec((1, d), const), pl.BlockSpec((1, d), const)],
        out_specs=pl.BlockSpec((tk // grp, grp, d), chunk3),
        scratch_shapes=[pltpu.VMEM((n_slots, d), _F32), pltpu.VMEM((_MOE_ROWS // grp, grp, d), _F32),
                        pltpu.SMEM((n_slots,), jnp.int32)],
    )
    return pl.pallas_call(
        functools.partial(_moe_kernel, tk),
        grid_spec=grid_spec,
        out_shape=jax.ShapeDtypeStruct((t // grp, grp, d), _F32),
        compiler_params=_params("arbitrary", "arbitrary"),
        name="moe",
    )(dest, offs, cnts, x, w1b.reshape(t // grp, grp, _LANES), w2b.reshape(t // grp, grp, _LANES),
      _mx(w_gate_up), _mx(w_down), ln_g.reshape(1, d), ln_b.reshape(1, d)).reshape(t, d)


def _layer0(x, h0, s5w, w_in, d_skip, w_glu, ln_g, ln_b, moe_w, tm, tk, tc_len):
    b, length, d = x.shape
    nbh = b // _BSUB
    bblk, cblk, a = s5w
    x2 = x.reshape(b * length, d)
    if length % tm == 0:
        nt = length // tm
        u_tm = _proj(x2, w_in, tm, [(0, d)], [_F32], grid=(b, nt), x_map=lambda s, j: (s * nt + j, 0),
                     out_shapes=[(nbh * length, _BSUB * d)],
                     out_maps=[lambda s, j: ((s // _BSUB) * nt + j, s % _BSUB)])[0]
        u_tm = u_tm.reshape(nbh, length * _BSUB, d)
    else:
        u = _proj(x2, w_in, b * length, [(0, d)], [_F32])[0]
        u_tm = jnp.transpose(u.reshape(nbh, _BSUB, length, d), (0, 2, 1, 3)).reshape(nbh, length * _BSUB, d)
    g_tm, h_t = _s5_scan(u_tm, bblk, cblk, a, d_skip, _state_to_blocks(h0), tc_len)
    tmx = min(tm, _MIX_ROWS)
    if length % tm == 0:
        nt = length // tmx
        f = g_tm.reshape(nbh * length, _BSUB * d)
        f_map = lambda i: (((i // nt) // _BSUB) * nt + i % nt, (i // nt) % _BSUB)
    else:
        f = jnp.transpose(g_tm.reshape(nbh, length, _BSUB, d), (0, 2, 1, 3)).reshape(b * length, d)
        f_map = lambda i: (i, 0)
    w_group, w_expert, w_gate_up, w_down = moe_w
    wr = _router_weight(w_group, w_expert)
    x1, route, w1b, w2b, cnt = _mix_ln_route(f, f_map, (tmx, d), x2, w_glu, ln_g[0], ln_b[0], wr, tmx, tk, True)
    x2o = _moe(x1, route, w1b, w2b, cnt, w_gate_up, w_down, ln_g[1], ln_b[1], tk)
    return x2o, _blocks_to_state(h_t)


_HALF = _LANES // 2
_N_PAIR = _N_KV // 2
_KV_HALF = _N_KV * _HEAD_DIM


def _head_perm():
    idx = []
    for p in range(_N_PAIR):
        for h in range(_HPG):
            for gs in range(2):
                base = ((2 * p + gs) * _HPG + h) * _HEAD_DIM
                idx.extend(range(base, base + _HEAD_DIM))
    return np.asarray(idx, np.int32)


def _gate_col(p, h, gs, branch):
    return ((2 * p + gs) * _HPG + h) * _N_BRANCH + branch


def _nsa_weights(kv_w, cmp_pe, cmp_w1, cmp_w2, w_qg, w_o):
    nq = _N_HEADS * _HEAD_DIM
    perm = _head_perm()
    w_q = w_qg[:, :nq][:, perm] * _ATTN_SCALE
    w_g = w_qg[:, nq:]
    w_g = jnp.pad(w_g, ((0, 0), (0, _LANES - w_g.shape[1])))
    eye = jnp.eye(2, dtype=_F32)
    w1r = cmp_w1.reshape(2, 2, _CMP_STRIDE, _HEAD_DIM, _CMP_HIDDEN)
    w1 = jnp.einsum("kzsdh,gG->ksgdzGh", w1r, eye).reshape(2, _CMP_STRIDE * _LANES, 4 * _CMP_HIDDEN)
    w2 = jnp.einsum("khd,gG->kghGd", cmp_w2, eye).reshape(2, 2 * _CMP_HIDDEN, _LANES)
    pe = jnp.pad(cmp_pe.reshape(2, 2, _CMP_STRIDE, _HEAD_DIM), ((0, 0),) * 3 + ((0, _LANES - _HEAD_DIM),))
    pe = pe.reshape(2, 2, 1, _CMP_STRIDE * _LANES)
    return dict(kv_w=kv_w, w_qg=jnp.concatenate([w_q, w_g], axis=1), w_o=w_o[perm], w1=_mx(w1), w2=_mx(w2), pe=pe)


def _sel_sum_matrix(n_rows, n_cmp, n_sel, lanes):
    m = np.zeros((n_rows, lanes), np.float32)
    for c in range(n_cmp):
        j0 = (c * _CMP_STRIDE) // _SEL_BLOCK
        j1 = (c * _CMP_STRIDE + _CMP_BLOCK - 1) // _SEL_BLOCK
        if j0 < n_sel:
            m[c, j0] += 1.0
        if j1 != j0 and j1 < n_sel:
            m[c, j1] += 1.0
    return jnp.asarray(m, _MXU_DTYPE)


def _expand_matrix(lanes, n_keys):
    j = np.arange(lanes)[:, None]
    key = np.arange(n_keys)[None, :]
    return jnp.asarray((key // _SEL_BLOCK == j).astype(np.float32), _MXU_DTYPE)


_CMP_PAGES = 16
_TB_PITCH = 24


def _compress_kernel(npg, transposed, pt_ref, *refs):
    page_refs = refs[:npg + 1]
    w1_ref, w2_ref, pe_ref, o_ref, lhs_ref, tbuf_ref = refs[npg + 1:]
    nlb = 2 * _N_PAIR
    page = page_refs[0].shape[1] if transposed else page_refs[0].shape[0]
    spp = page // _CMP_STRIDE
    seg = npg * spp + _SUBLANES
    rows = _N_PAIR * seg
    n_out = npg * spp
    pe_rows = (n_out + 2, n_out + 3)
    hid2 = 2 * _CMP_HIDDEN

    @pl.when((pl.program_id(0) == 0) & (pl.program_id(1) == 0))
    def _():
        lhs_ref[...] = jnp.zeros_like(lhs_ref)
        for k in range(2):
            for half in range(2):
                lhs_ref[k, pe_rows[half]:pe_rows[half] + 1, :] = pe_ref[k, half]

    def lane_block(ref, c):
        if transposed:
            return jnp.transpose(ref[c * _LANES:(c + 1) * _LANES, :])
        return ref[:, c * _LANES:(c + 1) * _LANES]

    for p in range(npg + 1):
        nsb = spp if p < npg else 1
        for c in range(nlb):
            k, pair = divmod(c, _N_PAIR)
            blk = lane_block(page_refs[p], c)
            for n in range(nsb):
                tbuf_ref[p, c, n * _TB_PITCH:n * _TB_PITCH + _CMP_STRIDE, :] = blk[n * _CMP_STRIDE:(n + 1) * _CMP_STRIDE]
            r0 = pair * seg + p * spp
            for s in range(_CMP_STRIDE):
                lhs_ref[k, r0:r0 + nsb, s * _LANES:(s + 1) * _LANES] = tbuf_ref[p, c, pl.ds(s, nsb, stride=_TB_PITCH), :]

    for k in range(2):
        res = _dot(lhs_ref[k], w1_ref[k])
        lo = res[:, :hid2]
        hi = pltpu.roll(res[:, hid2:], rows - 1, axis=0)
        pe = res[pe_rows[0]:pe_rows[0] + 1, :_CMP_HIDDEN] + res[pe_rows[1]:pe_rows[1] + 1, hid2:hid2 + _CMP_HIDDEN]
        out = _dot(jax.nn.gelu(lo + hi + jnp.tile(pe, (1, 2))), w2_ref[k])
        for pair in range(_N_PAIR):
            c = k * _N_PAIR + pair
            o_ref[:, c * _LANES:(c + 1) * _LANES] = out[pair * seg:pair * seg + n_out].astype(o_ref.dtype)


def _compress(pages, table, nsa, npg, transposed):
    b, npp = table.shape
    page, width = (pages.shape[2], pages.shape[1]) if transposed else pages.shape[1:]
    nlb = width // _LANES
    spp = page // _CMP_STRIDE
    seg = npg * spp + _SUBLANES
    kdim = _CMP_STRIDE * _LANES

    def page_map(i):
        return lambda s, j, pt: (pt[s * npp + jnp.minimum(j * npg + i, npp - 1)], 0, 0)

    const3 = lambda s, j, pt: (0, 0, 0)
    const4 = lambda s, j, pt: (0, 0, 0, 0)
    grid_spec = pltpu.PrefetchScalarGridSpec(
        num_scalar_prefetch=1,
        grid=(b, npp // npg),
        in_specs=[pl.BlockSpec((None,) + pages.shape[1:], page_map(i)) for i in range(npg + 1)]
        + [pl.BlockSpec((2, kdim, 4 * _CMP_HIDDEN), const3, pipeline_mode=pl.Buffered(1)),
           pl.BlockSpec((2, 2 * _CMP_HIDDEN, _LANES), const3), pl.BlockSpec((2, 2, 1, kdim), const4)],
        out_specs=pl.BlockSpec((None, npg * spp, width), lambda s, j, pt: (s, j, 0)),
        scratch_shapes=[pltpu.VMEM((2, _N_PAIR * seg, kdim), _F32),
                        pltpu.VMEM((npg + 1, nlb, spp * _TB_PITCH, _LANES), _F32)],
    )
    return pl.pallas_call(
        functools.partial(_compress_kernel, npg, transposed),
        grid_spec=grid_spec,
        out_shape=jax.ShapeDtypeStruct((b, npp * spp, width), _MXU_DTYPE),
        compiler_params=_params("arbitrary", "arbitrary"),
        name="nsa_compress",
    )(table.reshape(-1), *([pages] * (npg + 1)), nsa["w1"], nsa["w2"], nsa["pe"])


def _softmax_parts(s, mask):
    sm = jnp.where(mask, s, _NEG_INF)
    m = jnp.max(sm, -1, keepdims=True)
    e = jnp.where(mask, jnp.exp(sm - m), 0.0)
    return e, jnp.maximum(jnp.sum(e, -1, keepdims=True), _TINY)


def _dot_exact_rhs(a, b):
    a1 = _mx(a)
    r1 = a - a1.astype(_F32)
    a2 = _mx(r1)
    a3 = _mx(r1 - a2.astype(_F32))
    return _dot(a1, b) + (_dot(a2, b) + _dot(a3, b))


def _top_k_mask(score, k):
    lane = lax.broadcasted_iota(jnp.int32, score.shape, 1).astype(_F32)
    sel = jnp.zeros(score.shape, _F32)
    for _ in range(k):
        m = jnp.max(score, -1, keepdims=True)
        idx = jnp.min(jnp.where(score == m, lane, float(score.shape[1])), -1, keepdims=True)
        hit = lane == idx
        sel = jnp.where(hit, 1.0, sel)
        score = jnp.where(hit, -jnp.inf, score)
    return sel


def _select_mask(pg, mt_ref, q_pos, n_sel, k_sel):
    ps = _dot_exact_rhs(pg, mt_ref[...])
    j = lax.broadcasted_iota(jnp.int32, ps.shape, 1)
    valid = j * _SEL_BLOCK <= q_pos
    own = jnp.right_shift(q_pos, int(math.log2(_SEL_BLOCK)))
    forced = jnp.where(j == own, 1.0, jnp.where(j == 0, 1.0, 0.0))
    score = jnp.where(valid, ps + _FORCE_SCORE * forced, -1.0)
    score = jnp.where(j < n_sel, score, -jnp.inf)
    return _top_k_mask(score, k_sel)


def _masked_halves(blk):
    low = lax.broadcasted_iota(jnp.int32, blk.shape, 1) < _HALF
    return jnp.concatenate([jnp.where(low, blk, 0.0), jnp.where(low, 0.0, blk)], axis=0)


def _merge_halves(o, rows):
    low = lax.broadcasted_iota(jnp.int32, (rows, _LANES), 1) < _HALF
    return jnp.where(low, o[:rows], o[rows:])


def _gate_block(gates, p, h, branch):
    low = lax.broadcasted_iota(jnp.int32, (gates.shape[0], _LANES), 1) < _HALF
    c0, c1 = _gate_col(p, h, 0, branch), _gate_col(p, h, 1, branch)
    return jnp.where(low, gates[:, c0:c0 + 1], gates[:, c1:c1 + 1])


_KEY_CHUNK = 512


def _biased_attend(lq, k, v, bias):
    nk = k.shape[0]
    parts = [(c0, min(c0 + _KEY_CHUNK, nk)) for c0 in range(0, nk, _KEY_CHUNK)]
    s = [_dot_nt(lq, k[c0:c1]) + bias[:, c0:c1] for c0, c1 in parts]
    m = functools.reduce(jnp.maximum, [jnp.max(sc, -1, keepdims=True) for sc in s])
    den, acc = 0.0, 0.0
    for sc, (c0, c1) in zip(s, parts):
        e = jnp.exp(sc - m)
        den = den + jnp.sum(e, -1, keepdims=True)
        acc = acc + _dot(e, v[c0:c1])
    return acc * (1.0 / jnp.maximum(den, _TINY))


def _rank_select(score, n_sel, k_sel):
    st = jnp.transpose(score)
    top = -(-n_sel // _SUBLANES) * _SUBLANES
    s_top = st[:top]
    ridx = lax.broadcasted_iota(jnp.int32, s_top.shape, 0)
    ahead = jnp.zeros(s_top.shape, _F32)
    for j in range(n_sel):
        sj = st[j:j + 1, :]
        ahead = ahead + jnp.where(sj > s_top, 1.0, 0.0) + jnp.where(sj == s_top, jnp.where(ridx > j, 1.0, 0.0), 0.0)
    sel_t = jnp.where(ahead < k_sel, 1.0, 0.0)
    sel_t = jnp.concatenate([sel_t, jnp.zeros((st.shape[0] - top, st.shape[1]), _F32)], axis=0)
    return jnp.transpose(sel_t)


def _attn_prompt_kernel(tq, tile0, n_sel, k_sel, q_ref, gate_ref, ckv_ref, slc_ref, win_ref, mt_ref, ex_ref, o_ref):
    t0 = (tile0 + pl.program_id(1)) * tq
    nk = slc_ref.shape[0]
    nc = ckv_ref.shape[0]
    q_pos = t0 + lax.broadcasted_iota(jnp.int32, (tq, 1), 0)
    qf = q_ref[...].astype(_F32)
    gates = gate_ref[...]
    c_end = lax.broadcasted_iota(jnp.int32, (1, nc), 1) * _CMP_STRIDE + (_CMP_BLOCK - 1)
    cmask = c_end <= q_pos
    cmask2 = jnp.concatenate([cmask, cmask], axis=0)
    causal_bias = jnp.where(lax.broadcasted_iota(jnp.int32, (1, nk), 1) <= q_pos, 0.0, _NEG_INF)
    wk = min(nk, tq + _WINDOW)
    wstart = pl.multiple_of(jnp.clip(t0 - _WINDOW, 0, nk - wk), tq)
    dist = q_pos - (wstart + lax.broadcasted_iota(jnp.int32, (1, wk), 1))
    wbias = jnp.where(dist >= 0, jnp.where(dist < _WINDOW, 0.0, _NEG_INF), _NEG_INF)
    wbias2 = jnp.concatenate([wbias, wbias], axis=0)
    j = lax.broadcasted_iota(jnp.int32, (tq, _LANES), 1)
    valid = j * _SEL_BLOCK <= q_pos
    own = jnp.right_shift(q_pos, int(math.log2(_SEL_BLOCK)))
    forced = jnp.where(j == own, _FORCE_SCORE, jnp.where(j == 0, _FORCE_SCORE, 0.0))

    for p in range(_N_PAIR):
        ksl = slice(p * _LANES, (p + 1) * _LANES)
        vsl = slice(_KV_HALF + p * _LANES, _KV_HALF + (p + 1) * _LANES)
        kc, vc = ckv_ref[:, ksl], ckv_ref[:, vsl]
        lhs, o_cmp = [], []
        pg = [jnp.zeros((tq, nc), _F32), jnp.zeros((tq, nc), _F32)]
        for h in range(_HPG):
            blk = p * _HPG + h
            lq = _mx(_masked_halves(qf[:, blk * _LANES:(blk + 1) * _LANES]))
            lhs.append(lq)
            e, den = _softmax_parts(_dot_nt(lq, kc), cmask2)
            pc = e * (1.0 / den)
            o_cmp.append(_merge_halves(_dot(pc, vc), tq))
            pg = [pg[0] + pc[:tq], pg[1] + pc[tq:]]
        sbias = []
        for gs in range(2):
            ps = _dot_exact_rhs(pg[gs], mt_ref[...])
            score = jnp.where(valid, ps + forced, -1.0)
            score = jnp.where(j < n_sel, score, -jnp.inf)
            sel = _rank_select(score, n_sel, k_sel)
            picked = _dot(sel, ex_ref[...])
            sbias.append(jnp.minimum(causal_bias, (picked - 1.0) * (-_NEG_INF)))
        sbias2 = jnp.concatenate(sbias, axis=0)
        ks, vs = slc_ref[:, ksl], slc_ref[:, vsl]
        kw, vw = win_ref[pl.ds(wstart, wk), ksl], win_ref[pl.ds(wstart, wk), vsl]
        for h in range(_HPG):
            blk = p * _HPG + h
            o_slc = _merge_halves(_biased_attend(lhs[h], ks, vs, sbias2), tq)
            o_win = _merge_halves(_biased_attend(lhs[h], kw, vw, wbias2), tq)
            o = (_gate_block(gates, p, h, 0) * o_cmp[h] + _gate_block(gates, p, h, 1) * o_slc
                 + _gate_block(gates, p, h, 2) * o_win)
            o_ref[:, blk * _LANES:(blk + 1) * _LANES] = o.astype(o_ref.dtype)


def _nsa_prompt(x, b, length, nsa, tm):
    t, d = x.shape
    kvw = 2 * _KV_HALF
    cmp_f, slc_f, win_f, slc_b, win_b = _proj(
        x, nsa["kv_w"], tm, [(0, kvw), (kvw, 2 * kvw), (2 * kvw, 3 * kvw), (kvw, 2 * kvw), (2 * kvw, 3 * kvw)],
        [_F32, _F32, _F32, _MXU_DTYPE, _MXU_DTYPE])
    nq = _N_HEADS * _HEAD_DIM
    q, gates = _proj(x, nsa["w_qg"], tm, [(0, nq), (nq, nq + _LANES)], [_MXU_DTYPE, _F32], acts=[None, "sigmoid"])
    page = _LANES
    npp = length // page
    table = jnp.arange(b * npp, dtype=jnp.int32).reshape(b, npp)
    ckv = _compress(cmp_f.reshape(b * npp, page, kvw), table, nsa, min(_CMP_PAGES, npp), False)
    nc = length // _CMP_STRIDE
    tq = _LANES
    nq_t = length // tq
    mt = _sel_sum_matrix(nc, nc - 1, -(-length // _SEL_BLOCK), _LANES)
    slc3, win3 = slc_b.reshape(b, length, kvw), win_b.reshape(b, length, kvw)
    seq = lambda s, i: (s, 0, 0)
    const = lambda s, i: (0, 0)
    n_buckets = 4 if nq_t % 4 == 0 else 1
    tiles = nq_t // n_buckets
    outs = []
    for bk in range(n_buckets):
        tile0 = bk * tiles
        nk = (tile0 + tiles) * tq
        n_sel = nk // _SEL_BLOCK
        nkc = nk // _CMP_STRIDE
        q_map = lambda s, i, tile0=tile0: (s * nq_t + tile0 + i, 0)
        outs.append(pl.pallas_call(
            functools.partial(_attn_prompt_kernel, tq, tile0, n_sel, min(_N_SELECT, n_sel)),
            grid=(b, tiles),
            in_specs=[pl.BlockSpec((tq, nq), q_map), pl.BlockSpec((tq, _LANES), q_map),
                      pl.BlockSpec((None, nkc, kvw), seq), pl.BlockSpec((None, nk, kvw), seq),
                      pl.BlockSpec((None, nk, kvw), seq),
                      pl.BlockSpec((nkc, _LANES), const), pl.BlockSpec((_LANES, nk), const)],
            out_specs=pl.BlockSpec((None, tq, nq), lambda s, i: (s, i, 0)),
            out_shape=jax.ShapeDtypeStruct((b, tiles * tq, nq), _MXU_DTYPE),
            compiler_params=_params("arbitrary", "arbitrary"),
            name=f"nsa_prompt_attn_{bk}",
        )(q, gates, ckv, slc3, win3, mt[:nkc], _expand_matrix(_LANES, nk)))
    o = jnp.concatenate(outs, axis=1).reshape(t, nq)
    return o, cmp_f, slc_f, win_f


def _attn_sample_cmp_kernel(n_sel, k_sel, past, q_ref, ckv_ref, mt_ref, ocmp_ref, sel_ref):
    s_new = q_ref.shape[0]
    nc = ckv_ref.shape[0]
    q_pos = past + lax.broadcasted_iota(jnp.int32, (s_new, 1), 0)
    qf = q_ref[...].astype(_F32)
    c_end = lax.broadcasted_iota(jnp.int32, (1, nc), 1) * _CMP_STRIDE + (_CMP_BLOCK - 1)
    cmask = c_end <= q_pos
    cmask2 = jnp.concatenate([cmask, cmask], axis=0)
    pg_all = []
    for p in range(_N_PAIR):
        kc = ckv_ref[:, p * _LANES:(p + 1) * _LANES]
        vc = ckv_ref[:, _KV_HALF + p * _LANES:_KV_HALF + (p + 1) * _LANES]
        pg = [jnp.zeros((s_new, nc), _F32), jnp.zeros((s_new, nc), _F32)]
        for h in range(_HPG):
            blk = p * _HPG + h
            lq = _masked_halves(qf[:, blk * _LANES:(blk + 1) * _LANES])
            e, den = _softmax_parts(_dot_nt(lq, kc), cmask2)
            pc = e / den
            ocmp_ref[:, blk * _LANES:(blk + 1) * _LANES] = _merge_halves(_dot(pc, vc), s_new)
            pg = [pg[0] + pc[:s_new], pg[1] + pc[s_new:]]
        pg_all += pg
    sel = _select_mask(jnp.concatenate(pg_all, axis=0), mt_ref, jnp.tile(q_pos, (_N_KV, 1)), n_sel, k_sel)
    for g in range(_N_KV):
        sel_ref[g] = sel[g * s_new:(g + 1) * s_new]


def _attn_sample_kernel(npg, past, pt_ref, *refs):
    page_refs = refs[:npg]
    (q_ref, gate_ref, ocmp_ref, selc_ref, seln_ref, slcn_ref, winc_ref, winn_ref, ex_ref,
     o_ref, m_ref, l_ref, acc_ref) = refs[npg:]
    j = pl.program_id(1)
    s_new = q_ref.shape[0]
    rows = 2 * _HPG * s_new
    qf = q_ref[...].astype(_F32)

    @pl.when(j == 0)
    def _():
        m_ref[...] = jnp.full(m_ref.shape, _NEG_INF, _F32)
        l_ref[...] = jnp.zeros_like(l_ref)
        acc_ref[...] = jnp.zeros_like(acc_ref)

    def pair_lhs(p):
        halves = [_masked_halves(qf[:, (p * _HPG + h) * _LANES:(p * _HPG + h + 1) * _LANES]) for h in range(_HPG)]
        return jnp.concatenate([hv[:s_new] for hv in halves] + [hv[s_new:] for hv in halves], axis=0)

    def online(p, s, mask, pv):
        sm = jnp.where(mask, s, _NEG_INF)
        m_old = m_ref[p]
        m_new = jnp.maximum(m_old, jnp.max(sm, -1, keepdims=True))
        alpha = jnp.exp(m_old - m_new)
        e = jnp.where(mask, jnp.exp(sm - m_new), 0.0)
        l_ref[p] = alpha * l_ref[p] + jnp.sum(e, -1, keepdims=True)
        acc_ref[p] = alpha * acc_ref[p] + pv(e)
        m_ref[p] = m_new

    for p in range(_N_PAIR):
        ksl = slice(p * _LANES, (p + 1) * _LANES)
        vsl = slice(_KV_HALF + p * _LANES, _KV_HALF + (p + 1) * _LANES)
        lq = pair_lhs(p)
        k_t = jnp.concatenate([r[ksl, :] for r in page_refs], axis=1)
        v_t = jnp.concatenate([r[vsl, :] for r in page_refs], axis=1)
        mask = jnp.concatenate(
            [jnp.tile(_dot(selc_ref[2 * p + gs], ex_ref[...]) > 0.5, (_HPG, 1)) for gs in range(2)], axis=0)
        online(p, _dot(lq, k_t), mask, lambda e, v_t=v_t: _dot_nt(e, v_t))

    @pl.when(j == pl.num_programs(1) - 1)
    def _():
        gates = gate_ref[...]
        tok = lax.broadcasted_iota(jnp.int32, (s_new, 1), 0)
        q_pos = past + jnp.tile(tok, (2 * _HPG, 1))
        newer = lax.broadcasted_iota(jnp.int32, (1, s_new), 1) <= jnp.tile(tok, (2 * _HPG, 1))
        n_cache = winc_ref.shape[1]
        k_pos = past - n_cache + lax.broadcasted_iota(jnp.int32, (1, n_cache + s_new), 1)
        dist = q_pos - k_pos
        wmask = (dist >= 0) & (dist < _WINDOW) & (k_pos >= 0)
        wmask_c, wmask_n = wmask[:, :n_cache], wmask[:, n_cache:]
        for p in range(_N_PAIR):
            ksl = slice(p * _LANES, (p + 1) * _LANES)
            vsl = slice(_KV_HALF + p * _LANES, _KV_HALF + (p + 1) * _LANES)
            lq = pair_lhs(p)
            nmask = jnp.concatenate(
                [jnp.tile(seln_ref[2 * p + gs][:, :s_new] > 0.5, (_HPG, 1)) for gs in range(2)], axis=0) & newer
            online(p, _dot_nt(lq, slcn_ref[:, ksl]), nmask, lambda e, vsl=vsl: _dot(e, slcn_ref[:, vsl]))
            o_slc = acc_ref[p] / jnp.maximum(l_ref[p], _TINY)
            s_c = jnp.where(wmask_c, _dot(lq, winc_ref[ksl, :]), _NEG_INF)
            s_n = jnp.where(wmask_n, _dot_nt(lq, winn_ref[:, ksl]), _NEG_INF)
            m = jnp.maximum(jnp.max(s_c, -1, keepdims=True), jnp.max(s_n, -1, keepdims=True))
            e_c = jnp.where(wmask_c, jnp.exp(s_c - m), 0.0)
            e_n = jnp.where(wmask_n, jnp.exp(s_n - m), 0.0)
            den = jnp.maximum(jnp.sum(e_c, -1, keepdims=True) + jnp.sum(e_n, -1, keepdims=True), _TINY)
            o_win = (_dot_nt(e_c, winc_ref[vsl, :]) + _dot(e_n, winn_ref[:, vsl])) / den
            for h in range(_HPG):
                blk = p * _HPG + h
                lo, hi = h * s_new, (_HPG + h) * s_new
                pick = lambda o: _merge_halves(jnp.concatenate([o[lo:lo + s_new], o[hi:hi + s_new]], axis=0), s_new)
                o = (_gate_block(gates, p, h, 0) * ocmp_ref[:, blk * _LANES:(blk + 1) * _LANES]
                     + _gate_block(gates, p, h, 1) * pick(o_slc) + _gate_block(gates, p, h, 2) * pick(o_win))
                o_ref[:, blk * _LANES:(blk + 1) * _LANES] = o


def _nsa_sample(x, b, s_new, nsa, cache_cmp_kv, cache_slc_kv, cache_win_kv, page_table):
    t, d = x.shape
    kvw = 2 * _KV_HALF
    nq = _N_HEADS * _HEAD_DIM
    n_pool, page = cache_cmp_kv.shape[:2]
    npp = page_table.shape[1]
    past = npp * page
    assert (past + s_new) // _CMP_STRIDE == past // _CMP_STRIDE and past % _SEL_BLOCK == 0
    cmp_f, slc_f, win_f = _proj(x, nsa["kv_w"], t, [(0, kvw), (kvw, 2 * kvw), (2 * kvw, 3 * kvw)], [_F32] * 3)
    q, gates = _proj(x, nsa["w_qg"], t, [(0, nq), (nq, nq + _LANES)], [_F32, _F32], acts=[None, "sigmoid"])
    kv_t = lambda c: jnp.transpose(c, (0, 2, 3, 4, 1)).reshape(c.shape[0], kvw, c.shape[1])
    ckv = _compress(kv_t(cache_cmp_kv), page_table, nsa, min(_CMP_PAGES, npp), True)
    nc = past // _CMP_STRIDE
    n_sel = -(-(past + s_new) // _SEL_BLOCK)
    sel_lanes = -(-n_sel // _LANES) * _LANES
    mt = _sel_sum_matrix(nc, nc - 1, n_sel, sel_lanes)
    seq2 = lambda s: (s, 0)
    o_cmp, sel = pl.pallas_call(
        functools.partial(_attn_sample_cmp_kernel, n_sel, min(_N_SELECT, n_sel), past),
        grid=(b,),
        in_specs=[pl.BlockSpec((s_new, nq), seq2), pl.BlockSpec((None, nc, kvw), lambda s: (s, 0, 0)),
                  pl.BlockSpec((nc, sel_lanes), lambda s: (0, 0))],
        out_specs=[pl.BlockSpec((s_new, nq), seq2),
                   pl.BlockSpec((None, _N_KV, s_new, sel_lanes), lambda s: (s, 0, 0, 0))],
        out_shape=[jax.ShapeDtypeStruct((t, nq), _F32), jax.ShapeDtypeStruct((b, _N_KV, s_new, sel_lanes), _F32)],
        compiler_params=_params("arbitrary"),
    )(q, ckv, mt)

    npg = min(16, npp)
    n_chunks = npp // npg
    bpc = npg * page // _SEL_BLOCK
    n_past_blocks = past // _SEL_BLOCK
    selc = sel[..., :n_past_blocks].reshape(b, _N_KV, s_new, n_chunks, bpc)
    selc = jnp.pad(jnp.transpose(selc, (0, 3, 1, 2, 4)), ((0, 0),) * 4 + ((0, _LANES - bpc),))
    seln = jnp.broadcast_to(sel[..., n_past_blocks:n_past_blocks + 1], (b, _N_KV, s_new, _LANES))
    ex = _expand_matrix(_LANES, npg * page)
    n_cache = cache_win_kv.shape[1]

    def page_map(i):
        return lambda s, j, pt: (pt[s * npp + j * npg + i], 0, 0)

    seq = lambda s, j, pt: (s, 0)
    seq3 = lambda s, j, pt: (s, 0, 0)
    grid_spec = pltpu.PrefetchScalarGridSpec(
        num_scalar_prefetch=1,
        grid=(b, n_chunks),
        in_specs=[pl.BlockSpec((None, kvw, page), page_map(i)) for i in range(npg)]
        + [pl.BlockSpec((s_new, nq), seq), pl.BlockSpec((s_new, _LANES), seq), pl.BlockSpec((s_new, nq), seq),
           pl.BlockSpec((None, None, _N_KV, s_new, _LANES), lambda s, j, pt: (s, j, 0, 0, 0)),
           pl.BlockSpec((None, _N_KV, s_new, _LANES), lambda s, j, pt: (s, 0, 0, 0)),
           pl.BlockSpec((s_new, kvw), seq), pl.BlockSpec((None, kvw, n_cache), seq3), pl.BlockSpec((s_new, kvw), seq),
           pl.BlockSpec((_LANES, npg * page), lambda s, j, pt: (0, 0))],
        out_specs=pl.BlockSpec((s_new, nq), seq),
        scratch_shapes=[pltpu.VMEM((_N_PAIR, 2 * _HPG * s_new, 1), _F32), pltpu.VMEM((_N_PAIR, 2 * _HPG * s_new, 1), _F32),
                        pltpu.VMEM((_N_PAIR, 2 * _HPG * s_new, _LANES), _F32)],
    )
    o = pl.pallas_call(
        functools.partial(_attn_sample_kernel, npg, past),
        grid_spec=grid_spec,
        out_shape=jax.ShapeDtypeStruct((t, nq), _F32),
        compiler_params=_params("arbitrary", "arbitrary"),
        name="nsa_sample_attn",
    )(page_table.reshape(-1), *([kv_t(cache_slc_kv)] * npg), q, gates, o_cmp, selc, seln,
      slc_f, kv_t(cache_win_kv), win_f, ex)
    return o, cmp_f, slc_f, win_f


def kernel(x_prompt, x_sample, state_ssm, cache_cmp_kv, cache_slc_kv, cache_win_kv, page_table, a_w_in, a_lambda_re, a_lambda_im, a_log_dt, a_b_re, a_b_im, a_c_re, a_c_im, a_d, a_w_glu, kv_w, cmp_pe, cmp_w1, cmp_w2, b_w_qg, b_w_o, moe_w_group, moe_w_expert, moe_w_gate_up, moe_w_down, ln_g, ln_b):
    d = x_prompt.shape[-1]
    bp, lp, _ = x_prompt.shape
    bs, ls, _ = x_sample.shape
    tm_p, tk_p, tc_p = min(512, lp), min(2048, bp * lp), min(128, lp)
    ts = bs * ls

    ar, ai, bbr, bbi = _s5_discretize(a_lambda_re[0], a_lambda_im[0], a_log_dt[0], a_b_re[0], a_b_im[0])
    s5w = _s5_block_weights(ar, ai, bbr, bbi, a_c_re[0], a_c_im[0])
    moe0 = (moe_w_group[0], moe_w_expert[0], moe_w_gate_up[0], moe_w_down[0])
    h0p = jnp.zeros((bp,) + state_ssm.shape[2:], _F32)
    xp, hp = _layer0(x_prompt, h0p, s5w, a_w_in[0], a_d[0], a_w_glu[0], ln_g[0], ln_b[0], moe0, tm_p, tk_p, tc_p)
    xs, hs = _layer0(x_sample, state_ssm[0], s5w, a_w_in[0], a_d[0], a_w_glu[0], ln_g[0], ln_b[0], moe0, ts, ts, ls)

    nsa = _nsa_weights(kv_w, cmp_pe, cmp_w1, cmp_w2, b_w_qg[0], b_w_o[0])
    moe1 = (moe_w_group[1], moe_w_expert[1], moe_w_gate_up[1], moe_w_down[1])
    wr1 = _router_weight(moe1[0], moe1[1])
    kvw = 2 * _N_KV * _HEAD_DIM
    kv_shape = (2, _N_KV, _HEAD_DIM)

    op, cmp_p, slc_p, win_p = _nsa_prompt(xp, bp, lp, nsa, tm_p)
    tmx = min(tm_p, _MIX_ROWS)
    x1p, route, w1b, w2b, cnt = _mix_ln_route(op, lambda i: (i, 0), (tmx, d), xp, nsa["w_o"], ln_g[1, 0], ln_b[1, 0],
                                             wr1, tmx, tk_p, False)
    yp = _moe(x1p, route, w1b, w2b, cnt, moe1[2], moe1[3], ln_g[1, 1], ln_b[1, 1], tk_p)

    os_, cmp_s, slc_s, win_s = _nsa_sample(xs, bs, ls, nsa, cache_cmp_kv, cache_slc_kv, cache_win_kv, page_table)
    x1s, route, w1b, w2b, cnt = _mix_ln_route(os_, lambda i: (i, 0), (ts, d), xs, nsa["w_o"], ln_g[1, 0], ln_b[1, 0],
                                             wr1, ts, ts, False)
    ys = _moe(x1s, route, w1b, w2b, cnt, moe1[2], moe1[3], ln_g[1, 1], ln_b[1, 1], ts)

    n_keep = min(_WINDOW, lp)
    p_win = win_p.reshape((bp, lp) + kv_shape)[:, lp - n_keep:]
    win_all = jnp.concatenate([cache_win_kv, win_s.reshape((bs, ls) + kv_shape).astype(cache_win_kv.dtype)], axis=1)
    s_win = win_all[:, win_all.shape[1] - min(_WINDOW, win_all.shape[1]):]
    return (yp.reshape(bp, lp, d), ys.reshape(bs, ls, d), hp[None],
            cmp_p.reshape((bp, lp) + kv_shape), slc_p.reshape((bp, lp) + kv_shape), p_win,
            hs[None].astype(state_ssm.dtype),
            cmp_s.reshape((bs, ls) + kv_shape), slc_s.reshape((bs, ls) + kv_shape), s_win)
```
